```python
import jax
import jax.numpy as jnp
from jax import lax
import numpy as np

D_MODEL = 1024
BATCH = 4
SEQ = 4096
DEPTH = 2

GRID_W = 64
CTX_LEN = 256
N_MIXERS = 4
MIX_W = D_MODEL
GROUP_W = MIX_W // N_MIXERS
HEAD_DIM = 64
GROUP_HEADS = GROUP_W // HEAD_DIM
RMS_EPS = 1e-6
ROPE_BASE = 10000.0

RWKV_W_LORA = 32
RWKV_A_LORA = 32
RWKV_V_LORA = 32
RWKV_G_LORA = 64
RWKV_GN_EPS = 64e-5

RET_CHUNK = 128
GDN_CHUNK = 64
SHORT_CONV = 4
LRU_CONV = 4
LRU_C = 8.0

N_EXPERTS = 64
TOP_K = 8
EXPERT_FF = 256
SHARED_FF = 256
ROUTED_SCALE = 2.5
MOE_BLOCK = 128

RWKV_COLS = (GROUP_W, GROUP_W, GROUP_W, 2 * RWKV_W_LORA, 2 * RWKV_A_LORA, RWKV_G_LORA)
RET_COLS = (GROUP_W, GROUP_W, GROUP_W, GROUP_W)
GDN_COLS = (3 * GROUP_W, 2 * GROUP_HEADS, 2 * GROUP_HEADS, GROUP_W)
LRU_COLS = (GROUP_W, GROUP_W)
N_RWKV = sum(RWKV_COLS)
N_RET = sum(RET_COLS)
N_GDN = sum(GDN_COLS)
N_LRU = sum(LRU_COLS)
N_IN = N_RWKV + N_RET + N_GDN + N_LRU

kernel_name = 'hybrid_parallel_mixer_moe_dit'


def split_cols(t, sizes):
    return jnp.split(t, np.cumsum(sizes)[:-1].tolist(), axis=-1)


def to_heads(t):
    return t.reshape(t.shape[0], t.shape[1], -1, HEAD_DIM)


def flip_if(t, rev):
    return jnp.flip(t, 1) if rev else t


def rmsnorm(x, g):
    xf = x.astype(jnp.float32)
    y = xf * lax.rsqrt(jnp.mean(xf * xf, -1, keepdims=True) + RMS_EPS)
    return (y * g.astype(jnp.float32)).astype(x.dtype)


def head_rmsnorm(y):
    y = y.astype(jnp.float32)
    return y * lax.rsqrt(jnp.mean(y * y, -1, keepdims=True) + RMS_EPS)


def l2norm(t):
    tf = t.astype(jnp.float32)
    return tf * lax.rsqrt(jnp.sum(tf * tf, -1, keepdims=True) + 1e-6)


def group_norm_heads(y, w, b):
    mu = jnp.mean(y, -1, keepdims=True)
    var = jnp.mean(jnp.square(y - mu), -1, keepdims=True)
    yn = ((y - mu) * lax.rsqrt(var + RWKV_GN_EPS)).reshape(y.shape[0], y.shape[1], -1)
    return yn * w + b


def modulate(xn, shift, scale):
    return xn * (1 + scale) + shift


def conv_centred(x, w):
    K = w.shape[0]
    left = K // 2
    L = x.shape[1]
    xp = jnp.pad(x, ((0, 0), (left, K - 1 - left), (0, 0)))
    return sum(xp[:, j:j + L] * w[j] for j in range(K))


def token_shift_centred(x):
    xp = jnp.pad(x, ((0, 0), (1, 1), (0, 0)))
    return 0.5 * (xp[:, :-2] + xp[:, 2:])


def axial_rope(t):
    L = t.shape[1]
    rows_n = L // GRID_W
    rows = jnp.repeat(jnp.arange(rows_n), GRID_W).astype(jnp.float32)
    cols = jnp.tile(jnp.arange(GRID_W), rows_n).astype(jnp.float32)
    nf = HEAD_DIM // 4
    inv = ROPE_BASE ** (-jnp.arange(nf, dtype=jnp.float32) / nf)
    ang = jnp.concatenate([rows[:, None] * inv, cols[:, None] * inv], -1)[None, :, None, :]
    cos, sin = jnp.cos(ang), jnp.sin(ang)
    tf = t.astype(jnp.float32)
    t1, t2 = tf[..., :HEAD_DIM // 2], tf[..., HEAD_DIM // 2:]
    return jnp.concatenate([t1 * cos - t2 * sin, t1 * sin + t2 * cos], -1).astype(t.dtype)


def rwkv7_prepare(p, mu, w0, w2, a0, a2, g2, k_k, k_a):
    B_, L, _ = p.shape
    p = p + (token_shift_centred(p) - p) * mu
    r, k, v, wd, ad, gd = split_cols(p, RWKV_COLS)
    z = w0 + jnp.einsum('bldr,drc->bldc', jnp.tanh(wd.reshape(B_, L, 2, RWKV_W_LORA)), w2)
    log_w = -jnp.exp(-jax.nn.softplus(-z.astype(jnp.float32)) - 0.5)
    a = jax.nn.sigmoid(a0 + jnp.einsum('bldr,drc->bldc', ad.reshape(B_, L, 2, RWKV_A_LORA), a2))
    g = jax.nn.sigmoid(gd) @ g2
    kk = l2norm(to_heads(k * k_k))
    k_dir = k[:, :, None, :] * (1 + (a - 1) * k_a)
    return r, k_dir, v, log_w, a, kk, g


def rwkv7_scan(r, log_w, k, v, kk, a, S0, reverse):
    f = lambda t: jnp.moveaxis(t.astype(jnp.float32), 1, 0)

    def step(S, inp):
        r_t, lw_t, k_t, v_t, kk_t, a_t = inp
        sa = jnp.einsum('bhvk,bhk->bhv', S, kk_t)
        S = (S * jnp.exp(lw_t)[:, :, None, :] - sa[..., None] * (kk_t * a_t)[:, :, None, :]
             + v_t[..., None] * k_t[:, :, None, :])
        return S, jnp.einsum('bhvk,bhk->bhv', S, r_t)

    S, y = lax.scan(step, S0, (f(r), f(log_w), f(k), f(v), f(kk), f(a)), reverse=reverse)
    return jnp.moveaxis(y, 0, 1), S


def rwkv7_bonus(r, k, v, r_k):
    rf, kf, vf = (to_heads(t).astype(jnp.float32) for t in (r, k, v))
    b = jnp.sum(rf * kf * r_k, -1, keepdims=True) * vf
    return b.reshape(b.shape[0], b.shape[1], GROUP_W)


def rwkv7_mixer(pc, px, vres, mu, w0, w2, a0, a2, g2, k_k, k_a, r_k, ln_w, ln_b, ctx_out):
    prm = (mu, w0, w2, a0, a2, g2, k_k, k_a)
    rc, kc, vc, lwc, ac, kkc, gc = rwkv7_prepare(pc, *prm)
    rx, kx, vx, lwx, ax, kkx, gx = rwkv7_prepare(px, *prm)
    if vres is not None:
        vf_c, vf_x, vd_c, vd_x, v0, v2 = vres
        vc = vc + (vf_c - vc) * jax.nn.sigmoid(v0 + vd_c @ v2)
        vx = vx + (vf_x - vx) * jax.nn.sigmoid(v0 + vd_x @ v2)
    S0 = jnp.zeros((pc.shape[0], GROUP_HEADS, HEAD_DIM, HEAD_DIM), jnp.float32)
    yc = yx = bc = bx = 0.0
    for d, rev in enumerate((False, True)):
        zc, Sc = rwkv7_scan(to_heads(rc), to_heads(lwc[:, :, d]), to_heads(kc[:, :, d]), to_heads(vc),
                            kkc, to_heads(ac[:, :, d]), S0, rev)
        zx, _ = rwkv7_scan(to_heads(rx), to_heads(lwx[:, :, d]), to_heads(kx[:, :, d]), to_heads(vx),
                           kkx, to_heads(ax[:, :, d]), Sc, rev)
        yx = yx + zx
        bx = bx + rwkv7_bonus(rx, kx[:, :, d], vx, r_k)
        if ctx_out:
            yc = yc + zc
            bc = bc + rwkv7_bonus(rc, kc[:, :, d], vc, r_k)
    out_x = (group_norm_heads(yx, ln_w, ln_b) + bx) * gx
    out_c = (group_norm_heads(yc, ln_w, ln_b) + bc) * gc if ctx_out else None
    return out_c, out_x, vc, vx


def retention_chunked(q, k, v, log_gamma, R0):
    B_, L, H, dk = q.shape
    dv = v.shape[-1]
    C = RET_CHUNK
    n = L // C
    q, k, v = (t.astype(jnp.float32).reshape(B_, n, C, H, t.shape[-1]) for t in (q, k, v))
    i = jnp.arange(C, dtype=jnp.float32)
    diff = i[:, None] - i[None, :]
    decay = jnp.where(diff >= 0, jnp.exp(jnp.maximum(diff, 0.0)[None] * log_gamma[:, None, None]), 0.0)
    s = jnp.einsum('bnihd,bnjhd->bnhij', q, k) * decay
    o = jnp.einsum('bnhij,bnjhe->bnihe', s, v)
    zeta = jnp.exp((C - 1 - i)[None] * log_gamma[:, None])
    kv = jnp.einsum('bnjhd,bnjhe,hj->nbhde', k, v, zeta)
    gC = jnp.exp(C * log_gamma)[:, None, None]

    def step(R, kv_i):
        return R * gC + kv_i, R

    R, R_prev = lax.scan(step, R0, kv)
    xi = jnp.exp((i + 1)[None] * log_gamma[:, None])
    o = o + jnp.einsum('bnihd,nbhde,hi->bnihe', q, R_prev, xi)
    return o.reshape(B_, L, H, dv), R


def retention_mixer(pc, px, ret_lambda, ctx_out):
    def prep(p, rope):
        q, k, v, g = (to_heads(t) for t in split_cols(p, RET_COLS))
        if rope:
            q, k = axial_rope(q), axial_rope(k)
        return q, k * HEAD_DIM ** -0.5, v, g

    qc, kc, vc, gc = prep(pc, False)
    qx, kx, vx, gx = prep(px, True)
    R0 = jnp.zeros((pc.shape[0], GROUP_HEADS, HEAD_DIM, HEAD_DIM), jnp.float32)
    oc = ox = 0.0
    for d, rev in enumerate((False, True)):
        lg = -ret_lambda[d].astype(jnp.float32)
        o1, R = retention_chunked(flip_if(qc, rev), flip_if(kc, rev), flip_if(vc, rev), lg, R0)
        o2, _ = retention_chunked(flip_if(qx, rev), flip_if(kx, rev), flip_if(vx, rev), lg, R)
        ox = ox + flip_if(o2, rev)
        if ctx_out:
            oc = oc + flip_if(o1, rev)

    def post(o, g):
        y = head_rmsnorm(o) * jax.nn.silu(g.astype(jnp.float32))
        return y.reshape(y.shape[0], y.shape[1], GROUP_W)

    return (post(oc, gc) if ctx_out else None), post(ox, gx)


def gated_delta_chunked(q, k, v, g, beta, S0):
    B_, L, H, dk = q.shape
    dv = v.shape[-1]
    C = GDN_CHUNK
    n = L // C

    def chunk(t):
        t = t.astype(jnp.float32).reshape((B_, n, C, H) + t.shape[3:])
        return jnp.moveaxis(t, 3, 1)

    q, k, v, g, beta = chunk(q), chunk(k), chunk(v), chunk(g), chunk(beta)
    gc = jnp.cumsum(g, -1)
    i = jnp.arange(C)
    lower = i[:, None] >= i[None, :]
    strict = i[:, None] > i[None, :]
    gam = jnp.exp(jnp.where(lower, gc[..., :, None] - gc[..., None, :], -jnp.inf))
    kb = k * beta[..., None]
    A = jnp.where(strict, jnp.einsum('bhnid,bhnjd->bhnij', kb, k) * gam, 0.0)
    M = A + jnp.eye(C, dtype=jnp.float32)
    rhs = jnp.concatenate([v * beta[..., None], kb * jnp.exp(gc)[..., None]], -1)
    sol = lax.linalg.triangular_solve(M, rhs, left_side=True, lower=True, unit_diagonal=True)
    u, w = sol[..., :dv], sol[..., dv:]
    attn = jnp.einsum('bhnid,bhnjd->bhnij', q, k) * gam
    qg = q * jnp.exp(gc)[..., None]
    kg = k * jnp.exp(gc[..., -1:] - gc)[..., None]
    glast = jnp.exp(gc[..., -1])[..., None, None]

    def step(S, inp):
        u_i, w_i, a_i, qg_i, kg_i, gl_i = inp
        v_new = u_i - jnp.einsum('bhck,bhkv->bhcv', w_i, S)
        o_i = jnp.einsum('bhck,bhkv->bhcv', qg_i, S) + jnp.einsum('bhij,bhjv->bhiv', a_i, v_new)
        S = S * gl_i + jnp.einsum('bhck,bhcv->bhkv', kg_i, v_new)
        return S, o_i

    xs = tuple(jnp.moveaxis(t, 2, 0) for t in (u, w, attn, qg, kg, glast))
    S, o = lax.scan(step, S0, xs)
    o = jnp.transpose(o, (1, 0, 3, 2, 4)).reshape(B_, L, H, dv)
    return o, S


def gdn_mixer(pc, px, conv_w, a_log, dt_bias, norm_w, ctx_out):
    def prep(p):
        B_, L, _ = p.shape
        qkv, a, b, g = split_cols(p, GDN_COLS)
        q, k, v = jnp.split(jax.nn.silu(conv_centred(qkv, conv_w)), 3, -1)
        q = l2norm(to_heads(q)) * HEAD_DIM ** -0.5
        k = l2norm(to_heads(k))
        a = a.reshape(B_, L, 2, GROUP_HEADS).astype(jnp.float32)
        log_decay = -jnp.exp(a_log.astype(jnp.float32)) * jax.nn.softplus(a + dt_bias)
        beta = jax.nn.sigmoid(b.reshape(B_, L, 2, GROUP_HEADS).astype(jnp.float32))
        return q, k, to_heads(v), log_decay, beta, to_heads(g)

    qc, kc, vc, ldc, btc, gc = prep(pc)
    qx, kx, vx, ldx, btx, gx = prep(px)
    S0 = jnp.zeros((pc.shape[0], GROUP_HEADS, HEAD_DIM, HEAD_DIM), jnp.float32)
    oc = ox = 0.0
    for d, rev in enumerate((False, True)):
        o1, S = gated_delta_chunked(flip_if(qc, rev), flip_if(kc, rev), flip_if(vc, rev),
                                    flip_if(ldc[:, :, d], rev), flip_if(btc[:, :, d], rev), S0)
        o2, _ = gated_delta_chunked(flip_if(qx, rev), flip_if(kx, rev), flip_if(vx, rev),
                                    flip_if(ldx[:, :, d], rev), flip_if(btx[:, :, d], rev), S)
        ox = ox + flip_if(o2, rev)
        if ctx_out:
            oc = oc + flip_if(o1, rev)

    def post(o, g):
        y = head_rmsnorm(o) * norm_w * jax.nn.silu(g.astype(jnp.float32))
        return y.reshape(y.shape[0], y.shape[1], GROUP_W)

    return (post(oc, gc) if ctx_out else None), post(ox, gx)


def linear_scan(a, b, h0, reverse):
    if h0 is not None:
        end = -1 if reverse else 0
        b = b.at[:, end].add(a[:, end] * h0)

    def combine(l, r):
        return l[0] * r[0], r[0] * l[1] + r[1]

    return lax.associative_scan(combine, (a, b), reverse=reverse, axis=1)[1]


def rglru_mixer(pc, px, conv_w, conv_b, w_a, b_a, w_x, b_x, lam, ctx_out):
    def prep(p):
        xm, gate = split_cols(p, LRU_COLS)
        return conv_centred(xm, conv_w) + conv_b, gate

    def gates(xm, d):
        xb = to_heads(xm)
        r = jax.nn.sigmoid(jnp.einsum('blhi,hij->blhj', xb, w_a[d]).reshape(xm.shape) + b_a[d]).astype(jnp.float32)
        i = jax.nn.sigmoid(jnp.einsum('blhi,hij->blhj', xb, w_x[d]).reshape(xm.shape) + b_x[d]).astype(jnp.float32)
        log_a = -LRU_C * jax.nn.softplus(-lam[d].astype(jnp.float32)) * r
        return jnp.exp(log_a), jnp.sqrt(-jnp.expm1(2 * log_a)) * i * xm.astype(jnp.float32)

    xc, gc = prep(pc)
    xx, gx = prep(px)
    hc_sum = hx_sum = 0.0
    for d, rev in enumerate((False, True)):
        a_c, bb_c = gates(xc, d)
        h_c = linear_scan(a_c, bb_c, None, rev)
        a_x, bb_x = gates(xx, d)
        h_x = linear_scan(a_x, bb_x, h_c[:, 0] if rev else h_c[:, -1], rev)
        hx_sum = hx_sum + h_x
        if ctx_out:
            hc_sum = hc_sum + h_c
    out_x = hx_sum * jax.nn.gelu(gx.astype(jnp.float32))
    out_c = hc_sum * jax.nn.gelu(gc.astype(jnp.float32)) if ctx_out else None
    return out_c, out_x


def moe_ffn(h, router_w, router_bias, w_gate, w_up, w_down, s_gate, s_up, s_down):
    T, D = h.shape
    E = w_gate.shape[0]
    scores = jax.nn.sigmoid((h @ router_w).astype(jnp.float32))
    _, idx = lax.top_k(scores + router_bias.astype(jnp.float32), TOP_K)
    gate = jnp.take_along_axis(scores, idx, -1)
    gate = ROUTED_SCALE * gate / jnp.sum(gate, -1, keepdims=True)
    TK = T * TOP_K
    flat_e = idx.reshape(TK)
    order = jnp.argsort(flat_e)
    se = flat_e[order]
    counts = jnp.bincount(flat_e, length=E)
    nblk = (counts + MOE_BLOCK - 1) // MOE_BLOCK
    blk_end = jnp.cumsum(nblk)
    pad_start = (blk_end - nblk) * MOE_BLOCK
    grp_start = jnp.cumsum(counts) - counts
    dest = pad_start[se] + jnp.arange(TK) - grp_start[se]
    NB = -(-TK // MOE_BLOCK) + E
    slot_tok = jnp.full((NB * MOE_BLOCK,), T, jnp.int32).at[dest].set((order // TOP_K).astype(jnp.int32))
    slot_w = jnp.zeros((NB * MOE_BLOCK,), h.dtype).at[dest].set(gate.reshape(TK)[order].astype(h.dtype))
    blk_e = jnp.minimum(jnp.searchsorted(blk_end, jnp.arange(NB), side='right'), E - 1)
    h_pad = jnp.concatenate([h, jnp.zeros((1, D), h.dtype)], 0)

    def expert_block(acc, inp):
        tok, wt, e = inp
        hb = h_pad[tok]
        yb = (jax.nn.silu(hb @ w_gate[e]) * (hb @ w_up[e])) @ w_down[e]
        return acc.at[tok].add(yb * wt[:, None]), None

    routed, _ = lax.scan(expert_block, jnp.zeros((T + 1, D), h.dtype),
                         (slot_tok.reshape(NB, MOE_BLOCK), slot_w.reshape(NB, MOE_BLOCK), blk_e))
    shared = (jax.nn.silu(h @ s_gate) * (h @ s_up)) @ s_down
    return routed[:T] + shared


def setup_inputs(seed: int = 0) -> dict:
    key = jax.random.key(seed)
    keys = jax.random.split(key, 64)
    cnt = [0]
    f32 = jnp.float32

    def nxt():
        k = keys[cnt[0]]
        cnt[0] += 1
        return k

    def nrm(shape, scale=1.0):
        return scale * jax.random.normal(nxt(), shape, f32)

    def unif(shape, lo, hi):
        return jax.random.uniform(nxt(), shape, f32, lo, hi)

    L, H, D = DEPTH, GROUP_HEADS, D_MODEL
    ratio = jnp.arange(GROUP_W, dtype=f32) / (GROUP_W - 1)
    ret_base = -jnp.log(1.0 - 2.0 ** (-5.0 - jnp.arange(H, dtype=f32)))
    dt = jnp.exp(unif((L, 2, H), float(np.log(1e-3)), float(np.log(1e-1))))
    lru_a = unif((L, 2, GROUP_W), 0.9, 0.999) ** (1.0 / LRU_C)
    return {
        'x': nrm((BATCH, SEQ, D)),
        'c': nrm((BATCH, D)),
        'ctx': nrm((BATCH, CTX_LEN, D)),
        'c_ctx': nrm((D,)),
        'ada_w': nrm((L, D, 6 * D), 0.5 * D ** -0.5),
        'ada_b': nrm((L, 6 * D), 0.02),
        'norm_mix': 1.0 + nrm((L, D), 0.02),
        'norm_ffn': 1.0 + nrm((L, D), 0.02),
        'norm_final': 1.0 + nrm((D,), 0.02),
        'w_in': nrm((L, D, N_IN), D ** -0.5),
        'w_vres': nrm((L - 1, D, RWKV_V_LORA), D ** -0.5),
        'w_out': nrm((L, MIX_W, D), MIX_W ** -0.5),
        'rw_mu': unif((L, N_RWKV), 0.0, 1.0),
        'rw_w0': -6.0 + 5.0 * ratio ** 0.9 + nrm((L, 2, GROUP_W), 0.1),
        'rw_w2': nrm((L, 2, RWKV_W_LORA, GROUP_W), 0.1),
        'rw_a0': nrm((L, 2, GROUP_W), 0.1),
        'rw_a2': nrm((L, 2, RWKV_A_LORA, GROUP_W), 0.1),
        'rw_g2': nrm((L, RWKV_G_LORA, GROUP_W), RWKV_G_LORA ** -0.5),
        'rw_kk': 0.85 + nrm((L, GROUP_W), 0.02),
        'rw_ka': 1.0 + nrm((L, GROUP_W), 0.02),
        'rw_rk': nrm((L, H, HEAD_DIM), 0.1),
        'rw_ln_w': 1.0 + nrm((L, GROUP_W), 0.02),
        'rw_ln_b': nrm((L, GROUP_W), 0.02),
        'rw_v0': nrm((L - 1, GROUP_W), 0.1),
        'rw_v2': nrm((L - 1, RWKV_V_LORA, GROUP_W), 0.1),
        'ret_lambda': ret_base * jnp.exp(nrm((L, 2, H), 0.1)),
        'gdn_conv_w': nrm((L, SHORT_CONV, 3 * GROUP_W), SHORT_CONV ** -0.5),
        'gdn_a_log': jnp.log(unif((L, 2, H), 1.0, 16.0)),
        'gdn_dt_bias': dt + jnp.log(-jnp.expm1(-dt)),
        'gdn_norm_w': 1.0 + nrm((L, HEAD_DIM), 0.02),
        'lru_conv_w': nrm((L, LRU_CONV, GROUP_W), LRU_CONV ** -0.5),
        'lru_conv_b': nrm((L, GROUP_W), 0.02),
        'lru_w_a': nrm((L, 2, H, HEAD_DIM, HEAD_DIM), HEAD_DIM ** -0.5),
        'lru_b_a': nrm((L, 2, GROUP_W), 0.02),
        'lru_w_x': nrm((L, 2, H, HEAD_DIM, HEAD_DIM), HEAD_DIM ** -0.5),
        'lru_b_x': nrm((L, 2, GROUP_W), 0.02),
        'lru_lambda': jnp.log(lru_a) - jnp.log1p(-lru_a),
        'router_w': nrm((L, D, N_EXPERTS), D ** -0.5),
        'router_bias': nrm((L, N_EXPERTS), 0.01),
        'exp_w_gate': nrm((L, N_EXPERTS, D, EXPERT_FF), D ** -0.5),
        'exp_w_up': nrm((L, N_EXPERTS, D, EXPERT_FF), D ** -0.5),
        'exp_w_down': nrm((L, N_EXPERTS, EXPERT_FF, D), EXPERT_FF ** -0.5),
        'sh_w_gate': nrm((L, D, SHARED_FF), D ** -0.5),
        'sh_w_up': nrm((L, D, SHARED_FF), D ** -0.5),
        'sh_w_down': nrm((L, SHARED_FF, D), SHARED_FF ** -0.5),
    }


def reference(x, c, ctx, c_ctx, ada_w, ada_b, norm_mix, norm_ffn, norm_final, w_in, w_vres, w_out,
              rw_mu, rw_w0, rw_w2, rw_a0, rw_a2, rw_g2, rw_kk, rw_ka, rw_rk, rw_ln_w, rw_ln_b, rw_v0, rw_v2,
              ret_lambda, gdn_conv_w, gdn_a_log, gdn_dt_bias, gdn_norm_w,
              lru_conv_w, lru_conv_b, lru_w_a, lru_b_a, lru_w_x, lru_b_x, lru_lambda,
              router_w, router_bias, exp_w_gate, exp_w_up, exp_w_down, sh_w_gate, sh_w_up, sh_w_down):
    B_, L, D = x.shape
    n_lat = B_ * L
    vfirst = None
    for l in range(DEPTH):
        last = l == DEPTH - 1
        ctx_out = not last
        mod_x = [m[:, None, :] for m in jnp.split(jax.nn.silu(c) @ ada_w[l] + ada_b[l], 6, -1)]
        mod_c = jnp.split(jax.nn.silu(c_ctx) @ ada_w[l] + ada_b[l], 6, -1)
        hx = modulate(rmsnorm(x, norm_mix[l]), mod_x[0], mod_x[1])
        hc = modulate(rmsnorm(ctx, norm_mix[l]), mod_c[0], mod_c[1])
        if l == 0:
            w_l = w_in[l]
            cols = (N_RWKV, N_RET, N_GDN, N_LRU)
        else:
            w_l = jnp.concatenate([w_in[l], w_vres[l - 1]], -1)
            cols = (N_RWKV, N_RET, N_GDN, N_LRU, RWKV_V_LORA)
        px = split_cols(hx @ w_l, cols)
        pc = split_cols(hc @ w_l, cols)
        vres = None if l == 0 else (vfirst[0], vfirst[1], pc[4], px[4], rw_v0[l - 1], rw_v2[l - 1])
        a_c, a_x, v_c, v_x = rwkv7_mixer(pc[0], px[0], vres, rw_mu[l], rw_w0[l], rw_w2[l], rw_a0[l], rw_a2[l],
                                         rw_g2[l], rw_kk[l], rw_ka[l], rw_rk[l], rw_ln_w[l], rw_ln_b[l], ctx_out)
        if l == 0:
            vfirst = (v_c, v_x)
        b_c, b_x = retention_mixer(pc[1], px[1], ret_lambda[l], ctx_out)
        g_c, g_x = gdn_mixer(pc[2], px[2], gdn_conv_w[l], gdn_a_log[l], gdn_dt_bias[l], gdn_norm_w[l], ctx_out)
        r_c, r_x = rglru_mixer(pc[3], px[3], lru_conv_w[l], lru_conv_b[l], lru_w_a[l], lru_b_a[l],
                               lru_w_x[l], lru_b_x[l], lru_lambda[l], ctx_out)
        mix_x = jnp.concatenate([a_x, b_x, g_x, r_x], -1).astype(x.dtype) @ w_out[l]
        x = x + mod_x[2] * mix_x
        hx2 = modulate(rmsnorm(x, norm_ffn[l]), mod_x[3], mod_x[4]).reshape(n_lat, D)
        moe_prm = (router_w[l], router_bias[l], exp_w_gate[l], exp_w_up[l], exp_w_down[l],
                   sh_w_gate[l], sh_w_up[l], sh_w_down[l])
        if last:
            f_x = moe_ffn(hx2, *moe_prm)
        else:
            mix_c = jnp.concatenate([a_c, b_c, g_c, r_c], -1).astype(ctx.dtype) @ w_out[l]
            ctx = ctx + mod_c[2] * mix_c
            hc2 = modulate(rmsnorm(ctx, norm_ffn[l]), mod_c[3], mod_c[4]).reshape(-1, D)
            f = moe_ffn(jnp.concatenate([hx2, hc2], 0), *moe_prm)
            f_x = f[:n_lat]
            ctx = ctx + mod_c[5] * f[n_lat:].reshape(ctx.shape)
        x = x + mod_x[5] * f_x.reshape(x.shape)
    return rmsnorm(x, norm_final)
```

```python
import functools

import jax
import jax.numpy as jnp
import numpy as np
from jax import lax
from jax.experimental import pallas as pl
from jax.experimental.pallas import tpu as pltpu

F32, BF16, I32 = jnp.float32, jnp.bfloat16, jnp.int32

D_MODEL = 1024
GROUP_W = 256
HEAD_DIM = 64
N_HEADS = GROUP_W // HEAD_DIM
HEAD_SHIFT = 6
GRID_W = 64
ROPE_BASE = 10000.0
RMS_EPS = 1e-6
GN_EPS = 64e-5
LRU_C = 8.0
N_EXPERTS = 64
TOP_K = 8
EXPERT_FF = 256
ROUTED_SCALE = 2.5
MOE_BLOCK = 128
LORA_W = 32

TM = 256
CH = 64
HALO = 8
COMB_T = 64
VMEM_LIMIT_BYTES = 48 * 1024 * 1024

NN = (((1,), (0,)), ((), ()))
NT = (((1,), (1,)), ((), ()))
TN = (((0,), (0,)), ((), ()))


def _cparams(*sem):
    return pltpu.CompilerParams(dimension_semantics=sem, vmem_limit_bytes=VMEM_LIMIT_BYTES)


def _mm(a, b, dims=NN):
    return lax.dot_general(a, b, dims, preferred_element_type=F32)


def _mmb(a, b, dims=NN):
    return _mm(a.astype(BF16), b.astype(BF16), dims)


def _split2(a):
    hi = a.astype(BF16)
    return hi, (a - hi.astype(F32)).astype(BF16)


def _split3(a):
    hi = a.astype(BF16)
    r = a - hi.astype(F32)
    mid = r.astype(BF16)
    return hi, mid, (r - mid.astype(F32)).astype(BF16)


def _mm3(a, b, dims=NN):
    ah, al = _split2(a)
    bh, bl = _split2(b)
    return _mm(ah, bh, dims) + (_mm(ah, bl, dims) + _mm(al, bh, dims))


def _mmx(a, b_exact, dims=NN):
    ah, am, al = _split3(a)
    return _mm(ah, b_exact, dims) + (_mm(am, b_exact, dims) + _mm(al, b_exact, dims))


def _xmm(a_exact, b, dims=NN):
    bh, bm, bl = _split3(b)
    return _mm(a_exact, bh, dims) + (_mm(a_exact, bm, dims) + _mm(a_exact, bl, dims))


def _iota(shape, dim):
    return lax.broadcasted_iota(I32, shape, dim)


def _head_mask(rows, cols):
    return (_iota((rows, cols), 0) >> HEAD_SHIFT) == (_iota((rows, cols), 1) >> HEAD_SHIFT)


def _headsum(x, mbd_bf):
    return _mmx(x, mbd_bf)


def _bd(x, mask):
    return jnp.where(mask, jnp.concatenate([x] * N_HEADS, axis=0), 0.0)


def _fold(p):
    return (p[0:CH] + p[CH:2 * CH]) + (p[2 * CH:3 * CH] + p[3 * CH:4 * CH])


def _tri_inverse(a_all, mask):
    n = N_HEADS * CH
    a = _bd(a_all, mask)
    eye = (_iota((n, n), 0) == _iota((n, n), 1)).astype(F32)
    p = eye + a
    steps = CH.bit_length() - 2
    for _ in range(steps):
        a = _mm3(a, a)
        p = p + _mm3(p, a)
    return _fold(p)


def _order_masks(rev):
    t = _iota((CH, N_HEADS * CH), 0)
    j = _iota((CH, N_HEADS * CH), 1) & (CH - 1)
    if rev:
        return j > t, j >= t
    return j < t, j <= t


def _tri_ones(rev):
    t = _iota((CH, CH), 0)
    j = _iota((CH, CH), 1)
    return ((j >= t) if rev else (j <= t)).astype(BF16)


def _rms_modulate(x, g, shift, scale):
    y = x * lax.rsqrt(jnp.mean(x * x, axis=-1, keepdims=True) + RMS_EPS) * g
    return y * (1.0 + scale) + shift


def _softplus(x):
    return jnp.maximum(x, 0.0) + jnp.log(1.0 + jnp.exp(-jnp.abs(x)))


def _silu(x):
    return x * jax.nn.sigmoid(x)


def _gelu_tanh(x):
    return 0.5 * x * (1.0 + jnp.tanh(np.sqrt(2.0 / np.pi).astype(np.float32) * (x + 0.044715 * (x * x * x))))


def _shift_rows(p, prev_row, next_row):
    n = p.shape[0]
    r = _iota((n, 1), 0)
    up = jnp.where(r == 0, prev_row, pltpu.roll(p, 1, 0))
    dn = jnp.where(r == n - 1, next_row, pltpu.roll(p, n - 1, 0))
    return up, dn


def _seq_edges(i, n_ctx_tiles, n_tiles):
    first = jnp.logical_or(i == 0, i == n_ctx_tiles)
    last = jnp.logical_or(i == n_ctx_tiles - 1, i == n_tiles - 1)
    return first, last


def _fill_conv_scratch(scr, cur, prev8, next8, first, last):
    n = cur.shape[0]
    scr[0:HALO, :] = jnp.where(first, 0.0, prev8)
    scr[HALO:HALO + n, :] = cur
    scr[HALO + n:2 * HALO + n, :] = jnp.where(last, 0.0, next8)


def _conv4(scr, w, n):
    return (scr[HALO - 2:HALO - 2 + n, :] * w[0:1, :] + scr[HALO - 1:HALO - 1 + n, :] * w[1:2, :]
            + scr[HALO:HALO + n, :] * w[2:3, :] + scr[HALO + 1:HALO + 1 + n, :] * w[3:4, :])


def _ada_kernel(c_ref, w_ref, b_ref, o_ref):
    c = c_ref[...]
    o_ref[...] = _mm3(_silu(c), w_ref[...]) + b_ref[...]


def _ada_call(cvec, w, b):
    n = w.shape[1]
    tn = 768
    return pl.pallas_call(
        _ada_kernel,
        grid=(n // tn,),
        in_specs=[pl.BlockSpec((8, D_MODEL), lambda j: (0, 0)),
                  pl.BlockSpec((D_MODEL, tn), lambda j: (0, j)),
                  pl.BlockSpec((1, tn), lambda j: (0, j))],
        out_specs=pl.BlockSpec((8, tn), lambda j: (0, j)),
        out_shape=jax.ShapeDtypeStruct((8, n), F32),
        compiler_params=_cparams("arbitrary"),
        name="ada_mod",
    )(cvec, w, b)


IN_COLS = (1024, 1024, 1024, 512, 128)


def _inproj_kernel(x_ref, g_ref, mod_ref, w_ref, rw_ref, ret_ref, gdn_ref, lru_ref, ab_ref):
    h = _rms_modulate(x_ref[...], g_ref[...], mod_ref[0:1, :], mod_ref[1:2, :])
    p = _mm(h.astype(BF16), w_ref[...])
    o = 0
    for ref, n in zip((rw_ref, ret_ref, gdn_ref, lru_ref, ab_ref), IN_COLS):
        ref[...] = p[:, o:o + n]
        o += n


def _mod_spec(nc):
    return pl.BlockSpec((None, None, 6, D_MODEL), lambda b, i: (b, jnp.minimum(i // nc, 1), 0, 0))


def _inproj_call(xs, gain, mod, w, nc):
    B, S, _ = xs.shape
    n = w.shape[1]
    row = lambda width: pl.BlockSpec((None, TM, width), lambda b, i: (b, i, 0))
    return pl.pallas_call(
        _inproj_kernel,
        grid=(B, S // TM),
        in_specs=[row(D_MODEL), pl.BlockSpec((1, D_MODEL), lambda b, i: (0, 0)), _mod_spec(nc),
                  pl.BlockSpec((D_MODEL, n), lambda b, i: (0, 0))],
        out_specs=[row(c) for c in IN_COLS],
        out_shape=[jax.ShapeDtypeStruct((B, S, c), F32) for c in IN_COLS],
        compiler_params=_cparams("arbitrary", "arbitrary"),
        name="in_proj",
    )(xs, gain, mod, w)


def _rwkv_prep_kernel(nc, nt, has_vres, *refs):
    if has_vres:
        (p_ref, pp_ref, pn_ref, vf_ref, mu_ref, w0_ref, w2_ref, a0_ref, a2_ref, g2_ref, kkw_ref, ka_ref, rk_ref,
         v0_ref, v2_ref, sh_ref, dp_ref, bonus_ref, gate_ref) = refs
    else:
        (p_ref, pp_ref, pn_ref, mu_ref, w0_ref, w2_ref, a0_ref, a2_ref, g2_ref, kkw_ref, ka_ref, rk_ref,
         sh_ref, dp_ref, bonus_ref, gate_ref) = refs
    i = pl.program_id(1)
    first, last = _seq_edges(i, nc, nt)
    p = p_ref[...]
    prev_row = jnp.where(first, 0.0, pp_ref[HALO - 1:HALO, :])
    next_row = jnp.where(last, 0.0, pn_ref[0:1, :])
    up, dn = _shift_rows(p, prev_row, next_row)
    ps = p + (0.5 * (up + dn) - p) * mu_ref[...]
    mbd = _head_mask(GROUP_W, GROUP_W).astype(BF16)

    r = ps[:, 0:256]
    k = ps[:, 256:512]
    v = ps[:, 512:768]
    x1 = ps[:, 768:896]
    x2 = ps[:, 896:1024]
    z = w0_ref[...] + _mmb(jnp.tanh(x1), w2_ref[...])
    lw = -np.exp(-0.5).astype(np.float32) * jax.nn.sigmoid(z)
    a = jax.nn.sigmoid(a0_ref[...] + _mmb(x1, a2_ref[...]))
    gate_ref[...] = _mmb(jax.nn.sigmoid(x2), g2_ref[...])
    kk = k * kkw_ref[...]
    kk = kk * lax.rsqrt(_headsum(kk * kk, mbd) + 1e-6)
    if has_vres:
        v = v + (vf_ref[...] - v) * jax.nn.sigmoid(v0_ref[...] + _mmb(x2, v2_ref[...]))
    sh_ref[:, 0:256] = r
    sh_ref[:, 256:512] = kk
    sh_ref[:, 512:768] = v
    ksum = jnp.zeros_like(k)
    for d in range(2):
        a_d = a[:, d * 256:(d + 1) * 256]
        kd = k * (1.0 + (a_d - 1.0) * ka_ref[...])
        ksum = ksum + kd
        dp_ref[d, :, 0:256] = lw[:, d * 256:(d + 1) * 256]
        dp_ref[d, :, 256:512] = kd
        dp_ref[d, :, 512:768] = kk * a_d
    bonus_ref[...] = _headsum(r * ksum * rk_ref[...], mbd) * v


def _halo_specs(width, nt):
    per = TM // HALO
    cur = pl.BlockSpec((None, TM, width), lambda b, i: (b, i, 0))
    prev = pl.BlockSpec((None, HALO, width), lambda b, i: (b, jnp.maximum(i * per - 1, 0), 0))
    nxt = pl.BlockSpec((None, HALO, width), lambda b, i: (b, jnp.minimum((i + 1) * per, nt * per - 1), 0))
    return cur, prev, nxt


def _const_spec(a):
    nd = a.ndim
    return pl.BlockSpec(a.shape, lambda b, i: (0,) * nd)


def _rwkv_prep_call(p_rw, vfirst_pack, prm, nc):
    B, S, _ = p_rw.shape
    nt = S // TM
    has_vres = vfirst_pack is not None
    cur, prev, nxt = _halo_specs(1024, nt)
    ins = [p_rw, p_rw, p_rw]
    specs = [cur, prev, nxt]
    if has_vres:
        ins.append(vfirst_pack)
        specs.append(pl.BlockSpec((None, TM, 256), lambda b, i: (b, i, 2)))
    names = ["mu", "w0", "w2", "a0", "a2", "g2", "kk", "ka", "rk"] + (["v0", "v2"] if has_vres else [])
    for nme in names:
        ins.append(prm[nme])
        specs.append(_const_spec(prm[nme]))
    row = lambda w: pl.BlockSpec((None, TM, w), lambda b, i: (b, i, 0))
    return pl.pallas_call(
        functools.partial(_rwkv_prep_kernel, nc, nt, has_vres),
        grid=(B, nt),
        in_specs=specs,
        out_specs=[row(768), pl.BlockSpec((None, 2, TM, 768), lambda b, i: (b, 0, i, 0)), row(256), row(256)],
        out_shape=[jax.ShapeDtypeStruct((B, S, 768), F32), jax.ShapeDtypeStruct((B, 2, S, 768), F32),
                   jax.ShapeDtypeStruct((B, S, 256), F32), jax.ShapeDtypeStruct((B, S, 256), F32)],
        compiler_params=_cparams("arbitrary", "arbitrary"),
        name="rwkv_prep",
    )(*ins)


def _rwkv_chunk(sh, dp, s_ref, y_ref, rev, mask):
    r, kk, v = sh[:, 0:256], sh[:, 256:512], sh[:, 512:768]
    lw, kd, bb = dp[:, 0:256], dp[:, 256:512], dp[:, 512:768]
    g = _xmm(_tri_ones(rev), lw)
    g_tot = jnp.sum(lw, axis=0, keepdims=True)
    e_tot = jnp.exp(g_tot)
    eng = jnp.exp(-g)
    ab = -kk * jnp.exp(g - lw)
    bbar = bb * eng
    kbar = kd * eng
    rbar = r * jnp.exp(g)
    strict, incl = _order_masks(rev)
    sc = _mm3(jnp.concatenate([ab, rbar], axis=0),
              jnp.concatenate([_bd(bbar, mask), _bd(kbar, mask)], axis=0), NT)
    a_ab = jnp.where(strict, sc[0:CH, 0:256], 0.0)
    a_ak = jnp.where(strict, sc[0:CH, 256:512], 0.0)
    a_rb = jnp.where(incl, sc[CH:2 * CH, 0:256], 0.0)
    a_rk = jnp.where(incl, sc[CH:2 * CH, 256:512], 0.0)
    t_all = _tri_inverse(a_ab, mask)
    vbd = _bd(v, mask)
    rhs = _mm3(a_ak, vbd)
    wt = _mm3(t_all, _bd(ab, mask))
    u0 = _mm3(t_all, _bd(rhs, mask))
    y0 = _mm3(a_rk, vbd)
    s = s_ref[...]
    u = _mm3(wt, s, NT) + u0
    y_ref[...] = _mm3(rbar, s, NT) + _mm3(a_rb, _bd(u, mask)) + y0
    upd = _mm3(jnp.concatenate([u, v], axis=0), jnp.concatenate([bbar * e_tot, kbar * e_tot], axis=0), TN)
    s_ref[...] = s * e_tot + jnp.where(mask, upd, 0.0)


def _rwkv_scan_kernel(sh0_ref, sh1_ref, dp0_ref, dp1_ref, y0_ref, y1_ref, s0_ref, s1_ref):
    @pl.when(pl.program_id(1) == 0)
    def _():
        s0_ref[...] = jnp.zeros_like(s0_ref)
        s1_ref[...] = jnp.zeros_like(s1_ref)

    mask = _head_mask(GROUP_W, GROUP_W)
    _rwkv_chunk(sh0_ref[...], dp0_ref[...], s0_ref, y0_ref, False, mask)
    _rwkv_chunk(sh1_ref[...], dp1_ref[...], s1_ref, y1_ref, True, mask)


def _rev_idx(n, n_ctx, n_all):
    return jnp.where(n < n_ctx, n_ctx - 1 - n, n_all + n_ctx - 1 - n)


def _scan_specs(width_sh, width_dp, ncc, nch):
    fwd = lambda b, n: (b, n, 0)
    rev = lambda b, n: (b, _rev_idx(n, ncc, nch), 0)
    fwd_d = lambda b, n: (b, 0, n, 0)
    rev_d = lambda b, n: (b, 1, _rev_idx(n, ncc, nch), 0)
    return ([pl.BlockSpec((None, CH, width_sh), fwd), pl.BlockSpec((None, CH, width_sh), rev),
             pl.BlockSpec((None, None, CH, width_dp), fwd_d), pl.BlockSpec((None, None, CH, width_dp), rev_d)],
            [pl.BlockSpec((None, CH, GROUP_W), fwd), pl.BlockSpec((None, CH, GROUP_W), rev)])


def _state_scratch():
    return [pltpu.VMEM((GROUP_W, GROUP_W), F32), pltpu.VMEM((GROUP_W, GROUP_W), F32)]


def _rwkv_scan_call(sh, dp, ncc):
    B, S, _ = sh.shape
    nch = S // CH
    in_specs, out_specs = _scan_specs(768, 768, ncc, nch)
    return pl.pallas_call(
        _rwkv_scan_kernel,
        grid=(B, nch),
        in_specs=in_specs,
        out_specs=out_specs,
        out_shape=[jax.ShapeDtypeStruct((B, S, GROUP_W), F32)] * 2,
        scratch_shapes=_state_scratch(),
        compiler_params=_cparams("arbitrary", "arbitrary"),
        name="rwkv_scan",
    )(sh, sh, dp, dp)


def _ret_chunk(p, cs, lg, r_ref, o_ref, rev, mask):
    lane = _iota((CH, GROUP_W), 1)
    first_half = (lane & (HEAD_DIM - 1)) < HEAD_DIM // 2
    cos, sin = cs[:, 0:256], cs[:, 256:512]

    def rope(t):
        swapped = jnp.where(first_half, pltpu.roll(t, GROUP_W - HEAD_DIM // 2, 1), pltpu.roll(t, HEAD_DIM // 2, 1))
        return t * cos + swapped * sin

    q = rope(p[:, 0:256])
    k = rope(p[:, 256:512]) * (HEAD_DIM ** -0.5)
    v = p[:, 512:768]
    t4 = _iota((CH, N_HEADS * CH), 0)
    j4 = _iota((CH, N_HEADS * CH), 1) & (CH - 1)
    dist = (j4 - t4) if rev else (t4 - j4)
    decay = jnp.where(dist >= 0, jnp.exp(jnp.maximum(dist, 0).astype(F32) * lg), 0.0)
    tpos = _iota((CH, GROUP_W), 0)
    done = (CH - 1 - tpos) if rev else tpos
    xi = jnp.exp((done + 1).astype(F32) * lg)
    zeta = jnp.exp((CH - 1 - done).astype(F32) * lg)
    g_c = jnp.exp(float(CH) * lg)
    s = _mm3(q, _bd(k, mask), NT) * decay
    rr = r_ref[...]
    o_ref[...] = _mm3(s, _bd(v, mask)) + _mm3(q * xi, rr)
    r_ref[...] = rr * g_c + jnp.where(mask, _mm3(k * zeta, v, TN), 0.0)


def _ret_kernel(p0_ref, p1_ref, cs0_ref, cs1_ref, lg_ref, o0_ref, o1_ref, r0_ref, r1_ref):
    @pl.when(pl.program_id(1) == 0)
    def _():
        r0_ref[...] = jnp.zeros_like(r0_ref)
        r1_ref[...] = jnp.zeros_like(r1_ref)

    mask = _head_mask(GROUP_W, GROUP_W)
    _ret_chunk(p0_ref[...], cs0_ref[...], lg_ref[0:1, :], r0_ref, o0_ref, False, mask)
    _ret_chunk(p1_ref[...], cs1_ref[...], lg_ref[1:2, :], r1_ref, o1_ref, True, mask)


def _ret_call(p_ret, cs, lg, ncc):
    B, S, _ = p_ret.shape
    nch = S // CH
    fwd = lambda b, n: (b, n, 0)
    rev = lambda b, n: (b, _rev_idx(n, ncc, nch), 0)
    fwd2 = lambda b, n: (n, 0)
    rev2 = lambda b, n: (_rev_idx(n, ncc, nch), 0)
    return pl.pallas_call(
        _ret_kernel,
        grid=(B, nch),
        in_specs=[pl.BlockSpec((None, CH, 1024), fwd), pl.BlockSpec((None, CH, 1024), rev),
                  pl.BlockSpec((CH, 512), fwd2), pl.BlockSpec((CH, 512), rev2),
                  pl.BlockSpec((2, GROUP_W), lambda b, n: (0, 0))],
        out_specs=[pl.BlockSpec((None, CH, GROUP_W), fwd), pl.BlockSpec((None, CH, GROUP_W), rev)],
        out_shape=[jax.ShapeDtypeStruct((B, S, GROUP_W), F32)] * 2,
        scratch_shapes=_state_scratch(),
        compiler_params=_cparams("arbitrary", "arbitrary"),
        name="retention",
    )(p_ret, p_ret, cs, cs, lg)


def _gdn_prep_kernel(nc, nt, p_ref, pp_ref, pn_ref, ab_ref, cw_ref, alog_ref, dtb_ref, e_ref, sh_ref, dp_ref, scr):
    i = pl.program_id(1)
    first, last = _seq_edges(i, nc, nt)
    _fill_conv_scratch(scr, p_ref[:, 0:768], pp_ref[:, 0:768], pn_ref[:, 0:768], first, last)
    qkv = _silu(_conv4(scr, cw_ref[...], TM))
    mbd = _head_mask(GROUP_W, GROUP_W).astype(BF16)
    q, k = qkv[:, 0:256], qkv[:, 256:512]
    sh_ref[:, 0:256] = q * lax.rsqrt(_headsum(q * q, mbd) + 1e-6) * (HEAD_DIM ** -0.5)
    sh_ref[:, 256:512] = k * lax.rsqrt(_headsum(k * k, mbd) + 1e-6)
    sh_ref[:, 512:768] = qkv[:, 512:768]
    ab = ab_ref[...]
    ld = -jnp.exp(alog_ref[...]) * _softplus(ab + dtb_ref[...])
    vec = jnp.where(_iota(ab.shape, 1) < 2 * N_HEADS, ld, jax.nn.sigmoid(ab))
    ex = _mmx(vec, e_ref[...])
    for d in range(2):
        dp_ref[d, :, 0:256] = ex[:, d * 256:(d + 1) * 256]
        dp_ref[d, :, 256:512] = ex[:, 512 + d * 256:512 + (d + 1) * 256]


def _gdn_prep_call(p_gdn, p_ab, prm, nc):
    B, S, _ = p_gdn.shape
    nt = S // TM
    cur, prev, nxt = _halo_specs(1024, nt)
    consts = [prm["conv_w"], prm["alog"], prm["dtb"], prm["expand"]]
    row = lambda w: pl.BlockSpec((None, TM, w), lambda b, i: (b, i, 0))
    return pl.pallas_call(
        functools.partial(_gdn_prep_kernel, nc, nt),
        grid=(B, nt),
        in_specs=[cur, prev, nxt, row(128)] + [_const_spec(a) for a in consts],
        out_specs=[row(768), pl.BlockSpec((None, 2, TM, 512), lambda b, i: (b, 0, i, 0))],
        out_shape=[jax.ShapeDtypeStruct((B, S, 768), F32), jax.ShapeDtypeStruct((B, 2, S, 512), F32)],
        scratch_shapes=[pltpu.VMEM((TM + 2 * HALO, 768), F32)],
        compiler_params=_cparams("arbitrary", "arbitrary"),
        name="gdn_prep",
    )(p_gdn, p_gdn, p_gdn, p_ab, *consts)


def _gdn_chunk(sh, dp, s_ref, o_ref, rev, mask):
    q, k, v = sh[:, 0:256], sh[:, 256:512], sh[:, 512:768]
    ld, beta = dp[:, 0:256], dp[:, 256:512]
    gc = _xmm(_tri_ones(rev), ld)
    g_tot = jnp.sum(ld, axis=0, keepdims=True)
    t4 = _iota((CH, N_HEADS * CH), 0)
    j4 = _iota((CH, N_HEADS * CH), 1) & (CH - 1)
    gc_cols = jnp.sum(jnp.where(t4 == j4, gc, 0.0), axis=0, keepdims=True)
    strict, incl = _order_masks(rev)
    gam = jnp.where(incl, jnp.exp(jnp.minimum(gc - gc_cols, 0.0)), 0.0)
    kb = k * beta
    kk = _mm3(jnp.concatenate([kb, q], axis=0), _bd(k, mask), NT)
    a_mat = jnp.where(strict, kk[0:CH] * gam, 0.0)
    attn = kk[CH:2 * CH] * gam
    t_all = _tri_inverse(-a_mat, mask)
    egc = jnp.exp(gc)
    u = _mm3(t_all, _bd(v * beta, mask))
    w = _mm3(t_all, _bd(kb * egc, mask))
    s = s_ref[...]
    v_new = u - _mm3(w, s)
    o_ref[...] = _mm3(q * egc, s) + _mm3(attn, _bd(v_new, mask))
    upd = _mm3(k * jnp.exp(g_tot - gc), v_new, TN)
    s_ref[...] = s * jnp.exp(g_tot) + jnp.where(mask, upd, 0.0)


def _gdn_scan_kernel(sh0_ref, sh1_ref, dp0_ref, dp1_ref, o0_ref, o1_ref, s0_ref, s1_ref):
    @pl.when(pl.program_id(1) == 0)
    def _():
        s0_ref[...] = jnp.zeros_like(s0_ref)
        s1_ref[...] = jnp.zeros_like(s1_ref)

    mask = _head_mask(GROUP_W, GROUP_W)
    _gdn_chunk(sh0_ref[...], dp0_ref[...], s0_ref, o0_ref, False, mask)
    _gdn_chunk(sh1_ref[...], dp1_ref[...], s1_ref, o1_ref, True, mask)


def _gdn_scan_call(sh, dp, ncc):
    B, S, _ = sh.shape
    nch = S // CH
    in_specs, out_specs = _scan_specs(768, 512, ncc, nch)
    return pl.pallas_call(
        _gdn_scan_kernel,
        grid=(B, nch),
        in_specs=in_specs,
        out_specs=out_specs,
        out_shape=[jax.ShapeDtypeStruct((B, S, GROUP_W), F32)] * 2,
        scratch_shapes=_state_scratch(),
        compiler_params=_cparams("arbitrary", "arbitrary"),
        name="gdn_scan",
    )(sh, sh, dp, dp)


def _lru_tile(d, rev, first, last, p_ref, pp_ref, pn_ref, cw_ref, cb_ref, wa_ref, ba_ref, wx_ref, bx_ref, lam_ref,
              h_ref, carry_ref, scr):
    _fill_conv_scratch(scr, p_ref[:, 0:256], pp_ref[:, 0:256], pn_ref[:, 0:256], first, last)
    xm = _conv4(scr, cw_ref[...], TM) + cb_ref[...]
    xb = xm.astype(BF16)
    r = jax.nn.sigmoid(_mm(xb, wa_ref[d]) + ba_ref[d:d + 1, :])
    ig = jax.nn.sigmoid(_mm(xb, wx_ref[d]) + bx_ref[d:d + 1, :])
    log_a = -LRU_C * _softplus(-lam_ref[d:d + 1, :]) * r
    a = jnp.exp(log_a)
    th = jnp.tanh(log_a)
    b = jnp.sqrt(-2.0 * th / (1.0 - th)) * ig * xm
    row = _iota((TM, 1), 0)
    s = 1
    while s < TM:
        if rev:
            ok = row < TM - s
            a_n, b_n = pltpu.roll(a, TM - s, 0), pltpu.roll(b, TM - s, 0)
        else:
            ok = row >= s
            a_n, b_n = pltpu.roll(a, s, 0), pltpu.roll(b, s, 0)
        b = jnp.where(ok, b + a * b_n, b)
        a = jnp.where(ok, a * a_n, a)
        s *= 2
    h = b + a * carry_ref[0:1, :]
    h_ref[...] = h
    end = h[0:1, :] if rev else h[TM - 1:TM, :]
    carry_ref[...] = jnp.broadcast_to(end, carry_ref.shape)


def _lru_kernel(nc, nt, p0, pp0, pn0, p1, pp1, pn1, cw, cb, wa, ba, wx, bx, lam, h0_ref, h1_ref, c0, c1, scr):
    i = pl.program_id(1)

    @pl.when(i == 0)
    def _():
        c0[...] = jnp.zeros_like(c0)
        c1[...] = jnp.zeros_like(c1)

    first, last = _seq_edges(i, nc, nt)
    _lru_tile(0, False, first, last, p0, pp0, pn0, cw, cb, wa, ba, wx, bx, lam, h0_ref, c0, scr)
    j = _rev_idx(i, nc, nt)
    first, last = _seq_edges(j, nc, nt)
    _lru_tile(1, True, first, last, p1, pp1, pn1, cw, cb, wa, ba, wx, bx, lam, h1_ref, c1, scr)


def _lru_call(p_lru, prm, nc):
    B, S, _ = p_lru.shape
    nt = S // TM
    per = TM // HALO
    rj = lambda i: _rev_idx(i, nc, nt)
    cur, prev, nxt = _halo_specs(512, nt)
    cur_r = pl.BlockSpec((None, TM, 512), lambda b, i: (b, rj(i), 0))
    prev_r = pl.BlockSpec((None, HALO, 512), lambda b, i: (b, jnp.maximum(rj(i) * per - 1, 0), 0))
    nxt_r = pl.BlockSpec((None, HALO, 512), lambda b, i: (b, jnp.minimum((rj(i) + 1) * per, nt * per - 1), 0))
    consts = [prm[k] for k in ("conv_w", "conv_b", "wa", "ba", "wx", "bx", "lam")]
    return pl.pallas_call(
        functools.partial(_lru_kernel, nc, nt),
        grid=(B, nt),
        in_specs=[cur, prev, nxt, cur_r, prev_r, nxt_r] + [_const_spec(a) for a in consts],
        out_specs=[pl.BlockSpec((None, TM, GROUP_W), lambda b, i: (b, i, 0)),
                   pl.BlockSpec((None, TM, GROUP_W), lambda b, i: (b, rj(i), 0))],
        out_shape=[jax.ShapeDtypeStruct((B, S, GROUP_W), F32)] * 2,
        scratch_shapes=[pltpu.VMEM((8, GROUP_W), F32), pltpu.VMEM((8, GROUP_W), F32),
                        pltpu.VMEM((TM + 2 * HALO, GROUP_W), F32)],
        compiler_params=_cparams("arbitrary", "arbitrary"),
        name="rglru",
    )(p_lru, p_lru, p_lru, p_lru, p_lru, p_lru, *consts)


def _mix_out_kernel(y0, y1, bonus, grw, ro0, ro1, gret, go0, go1, ggdn, lh0, lh1, glru, x_ref, mod_ref, w_ref,
                    lnw, lnb, gnw, o_ref):
    mbd = _head_mask(GROUP_W, GROUP_W).astype(BF16)
    inv = 1.0 / HEAD_DIM

    def head_rms(o):
        return o * lax.rsqrt(_headsum(o * o, mbd) * inv + RMS_EPS)

    y = y0[...] + y1[...]
    mu = _headsum(y, mbd) * inv
    dy = y - mu
    var = _headsum(dy * dy, mbd) * inv
    a = (dy * lax.rsqrt(var + GN_EPS) * lnw[...] + lnb[...] + bonus[...]) * grw[...]
    b = head_rms(ro0[...] + ro1[...]) * _silu(gret[...])
    g = head_rms(go0[...] + go1[...]) * gnw[...] * _silu(ggdn[...])
    r = (lh0[...] + lh1[...]) * _gelu_tanh(glru[...])
    w = w_ref[...]
    mix = (_mm(a.astype(BF16), w[0:256]) + _mm(b.astype(BF16), w[256:512])
           + _mm(g.astype(BF16), w[512:768]) + _mm(r.astype(BF16), w[768:1024]))
    o_ref[...] = x_ref[...] + mod_ref[2:3, :] * mix


def _mix_out_call(ys, bonus, grw, ros, p_ret, gos, p_gdn, lhs, p_lru, xs, mod, w_out, lnw, lnb, gnw, nc):
    B, S, _ = xs.shape
    row = lambda w, c=0: pl.BlockSpec((None, TM, w), lambda b, i: (b, i, c))
    g = row(GROUP_W)
    consts = [lnw, lnb, gnw]
    return pl.pallas_call(
        _mix_out_kernel,
        grid=(B, S // TM),
        in_specs=[g, g, g, g, g, g, row(GROUP_W, 3), g, g, row(GROUP_W, 3), g, g, row(GROUP_W, 1),
                  row(D_MODEL), _mod_spec(nc), pl.BlockSpec((D_MODEL, D_MODEL), lambda b, i: (0, 0))]
                 + [_const_spec(a) for a in consts],
        out_specs=row(D_MODEL),
        out_shape=jax.ShapeDtypeStruct((B, S, D_MODEL), F32),
        compiler_params=_cparams("arbitrary", "arbitrary"),
        name="mix_out",
    )(ys[0], ys[1], bonus, grw, ros[0], ros[1], p_ret, gos[0], gos[1], p_gdn, lhs[0], lhs[1], p_lru,
      xs, mod, w_out, *consts)


def _moe_pre_kernel(x_ref, g_ref, mod_ref, rw_ref, rb_ref, sg_ref, su_ref, sd_ref, h_ref, idx_ref, gate_ref, sh_ref):
    h = _rms_modulate(x_ref[...], g_ref[...], mod_ref[3:4, :], mod_ref[4:5, :])
    h_ref[...] = h
    scores = jax.nn.sigmoid(_mm3(h, rw_ref[...]))
    sel = scores + rb_ref[...]
    lane = _iota(scores.shape, 1)
    gates = jnp.zeros_like(scores)
    idxs = jnp.zeros(scores.shape, I32)
    total = jnp.zeros((scores.shape[0], 1), F32)
    for kk in range(TOP_K):
        m = jnp.max(sel, axis=-1, keepdims=True)
        idx = jnp.min(jnp.where(sel == m, lane, 2 * N_EXPERTS), axis=-1, keepdims=True)
        hit = lane == idx
        gk = jnp.sum(jnp.where(hit, scores, 0.0), axis=-1, keepdims=True)
        total = total + gk
        gates = jnp.where(lane == kk, gk, gates)
        idxs = jnp.where(lane == kk, idx, idxs)
        sel = jnp.where(hit, -jnp.inf, sel)
    gate_ref[...] = ROUTED_SCALE * gates / total
    idx_ref[...] = idxs
    hb = h.astype(BF16)
    act = _silu(_mm(hb, sg_ref[...])) * _mm(hb, su_ref[...])
    sh_ref[...] = _mm(act.astype(BF16), sd_ref[...])


def _moe_pre_call(xs, gain, mod, rw, rb, sg, su, sd, nc):
    B, S, _ = xs.shape
    row = lambda w: pl.BlockSpec((None, TM, w), lambda b, i: (b, i, 0))
    consts = [rw, rb, sg, su, sd]
    return pl.pallas_call(
        _moe_pre_kernel,
        grid=(B, S // TM),
        in_specs=[row(D_MODEL), pl.BlockSpec((1, D_MODEL), lambda b, i: (0, 0)), _mod_spec(nc)]
                 + [_const_spec(a) for a in consts],
        out_specs=[row(D_MODEL), row(128), row(128), row(D_MODEL)],
        out_shape=[jax.ShapeDtypeStruct((B, S, D_MODEL), F32), jax.ShapeDtypeStruct((B, S, 128), I32),
                   jax.ShapeDtypeStruct((B, S, 128), F32), jax.ShapeDtypeStruct((B, S, D_MODEL), F32)],
        compiler_params=_cparams("arbitrary", "arbitrary"),
        name="moe_pre",
    )(xs, gain, mod, *consts)


def _row_copy(src_hbm, dst, sem, src_row, dst_row):
    return pltpu.make_async_copy(src_hbm.at[pl.ds(src_row, 1), :], dst.at[pl.ds(dst_row, 1), :], sem)


def _expert_kernel(blk_e_ref, nused_ref, tok_ref, h_hbm, wg_ref, wu_ref, wd_ref, y_ref, hb, sem):
    i = pl.program_id(0)

    @pl.when(i < nused_ref[0])
    def _():
        def start(r, c):
            _row_copy(h_hbm, hb, sem, tok_ref[0, 0, r], r).start()
            return c

        lax.fori_loop(0, MOE_BLOCK, start, 0)

        def wait(r, c):
            _row_copy(h_hbm, hb, sem, 0, r).wait()
            return c

        lax.fori_loop(0, MOE_BLOCK, wait, 0)
        x = hb[...].astype(BF16)
        act = _silu(_mm(x, wg_ref[...])) * _mm(x, wu_ref[...])
        y_ref[...] = _mm(act.astype(BF16), wd_ref[...])

    @pl.when(i >= nused_ref[0])
    def _():
        y_ref[...] = jnp.zeros_like(y_ref)


def _expert_call(blk_e, nused, slot_tok, h_flat, wg, wu, wd):
    nb = blk_e.shape[0]
    ff = wg.shape[2]
    grid_spec = pltpu.PrefetchScalarGridSpec(
        num_scalar_prefetch=2,
        grid=(nb,),
        in_specs=[pl.BlockSpec((1, 1, MOE_BLOCK), lambda i, be, nu: (i, 0, 0), memory_space=pltpu.SMEM),
                  pl.BlockSpec(memory_space=pl.ANY),
                  pl.BlockSpec((None, D_MODEL, ff), lambda i, be, nu: (be[i], 0, 0)),
                  pl.BlockSpec((None, D_MODEL, ff), lambda i, be, nu: (be[i], 0, 0)),
                  pl.BlockSpec((None, ff, D_MODEL), lambda i, be, nu: (be[i], 0, 0))],
        out_specs=pl.BlockSpec((MOE_BLOCK, D_MODEL), lambda i, be, nu: (i, 0)),
        scratch_shapes=[pltpu.VMEM((MOE_BLOCK, D_MODEL), F32), pltpu.SemaphoreType.DMA(())],
    )
    return pl.pallas_call(
        _expert_kernel,
        grid_spec=grid_spec,
        out_shape=jax.ShapeDtypeStruct((nb * MOE_BLOCK, D_MODEL), F32),
        compiler_params=_cparams("arbitrary"),
        name="moe_experts",
    )(blk_e, nused, slot_tok, h_flat, wg, wu, wd)


def _combine_kernel(pos_ref, y_hbm, gate_ref, sh_ref, x_ref, mod_ref, o_ref, buf, sem):
    n = COMB_T * TOP_K

    def start(j, c):
        t = j // TOP_K
        k = j - t * TOP_K
        _row_copy(y_hbm, buf, sem, pos_ref[0, 0, j], k * COMB_T + t).start()
        return c

    lax.fori_loop(0, n, start, 0)

    def wait(j, c):
        _row_copy(y_hbm, buf, sem, 0, j).wait()
        return c

    lax.fori_loop(0, n, wait, 0)
    gate = gate_ref[...]
    routed = jnp.zeros((COMB_T, D_MODEL), F32)
    for k in range(TOP_K):
        routed = routed + gate[:, k:k + 1] * buf[k * COMB_T:(k + 1) * COMB_T, :]
    o_ref[...] = x_ref[...] + mod_ref[5:6, :] * (routed + sh_ref[...])


def _combine_call(pos, y, gate, shared, xs, mod, nc):
    B, S, _ = xs.shape
    per = S // COMB_T
    row = lambda w: pl.BlockSpec((None, COMB_T, w), lambda b, i: (b, i, 0))
    mod_spec = pl.BlockSpec((None, None, 6, D_MODEL),
                            lambda b, i: (b, jnp.minimum(i // (nc * TM // COMB_T), 1), 0, 0))
    return pl.pallas_call(
        _combine_kernel,
        grid=(B, per),
        in_specs=[pl.BlockSpec((1, 1, COMB_T * TOP_K), lambda b, i: (b * per + i, 0, 0), memory_space=pltpu.SMEM),
                  pl.BlockSpec(memory_space=pl.ANY), row(128), row(D_MODEL), row(D_MODEL), mod_spec],
        out_specs=row(D_MODEL),
        out_shape=jax.ShapeDtypeStruct((B, S, D_MODEL), F32),
        scratch_shapes=[pltpu.VMEM((COMB_T * TOP_K, D_MODEL), F32), pltpu.SemaphoreType.DMA(())],
        compiler_params=_cparams("arbitrary", "arbitrary"),
        name="moe_combine",
    )(pos, y, gate, shared, xs, mod)


def _final_norm_kernel(x_ref, g_ref, o_ref):
    x = x_ref[...]
    o_ref[...] = x * lax.rsqrt(jnp.mean(x * x, axis=-1, keepdims=True) + RMS_EPS) * g_ref[...]


def _final_norm_call(xs, gain, nc):
    B, S, _ = xs.shape
    tx = S - nc * TM
    return pl.pallas_call(
        _final_norm_kernel,
        grid=(B, tx // TM),
        in_specs=[pl.BlockSpec((None, TM, D_MODEL), lambda b, i: (b, i + nc, 0)),
                  pl.BlockSpec((1, D_MODEL), lambda b, i: (0, 0))],
        out_specs=pl.BlockSpec((None, TM, D_MODEL), lambda b, i: (b, i, 0)),
        out_shape=jax.ShapeDtypeStruct((B, tx, D_MODEL), F32),
        compiler_params=_cparams("arbitrary", "arbitrary"),
        name="final_norm",
    )(xs, gain)


N_RWKV, N_RET, N_GDN, N_LRU = 960, 1024, 1040, 512


def _pack_w_in(w_in_l, w_vres_l):
    o_ret = N_RWKV
    o_gdn = o_ret + N_RET
    o_lru = o_gdn + N_GDN
    z = lambda n: jnp.zeros((D_MODEL, n), F32)
    vres = z(LORA_W) if w_vres_l is None else w_vres_l
    cols = [w_in_l[:, 0:N_RWKV], vres, z(1024 - N_RWKV - LORA_W),
            w_in_l[:, o_ret:o_gdn],
            w_in_l[:, o_gdn:o_gdn + 768], w_in_l[:, o_gdn + 784:o_lru],
            w_in_l[:, o_lru:o_lru + N_LRU],
            w_in_l[:, o_gdn + 768:o_gdn + 784], z(112)]
    return jnp.concatenate(cols, axis=1).astype(BF16)


def _block_diag2(w):
    r, c = w.shape[1], w.shape[2]
    z = jnp.zeros((r, c), F32)
    return jnp.concatenate([jnp.concatenate([w[0], z], 1), jnp.concatenate([z, w[1]], 1)], 0)


def _rwkv_params(l, rw_mu, rw_w0, rw_w2, rw_a0, rw_a2, rw_g2, rw_kk, rw_ka, rw_rk, rw_v0, rw_v2):
    z64 = jnp.zeros((2 * LORA_W, 2 * GROUP_W), F32)
    prm = {
        "mu": jnp.pad(rw_mu[l], (0, 1024 - N_RWKV))[None, :],
        "w0": rw_w0[l].reshape(1, 2 * GROUP_W),
        "w2": jnp.concatenate([_block_diag2(rw_w2[l]), z64], 0),
        "a0": rw_a0[l].reshape(1, 2 * GROUP_W),
        "a2": jnp.concatenate([z64, _block_diag2(rw_a2[l])], 0),
        "g2": jnp.concatenate([rw_g2[l], jnp.zeros((64, GROUP_W), F32)], 0),
        "kk": rw_kk[l][None, :],
        "ka": rw_ka[l][None, :],
        "rk": rw_rk[l].reshape(1, GROUP_W),
    }
    if l > 0:
        prm["v0"] = rw_v0[l - 1][None, :]
        prm["v2"] = jnp.concatenate([jnp.zeros((64, GROUP_W), F32), rw_v2[l - 1],
                                     jnp.zeros((128 - 64 - LORA_W, GROUP_W), F32)], 0)
    return prm


def _head_expand():
    e = np.zeros((128, 1024), np.float32)
    for c in range(4 * N_HEADS):
        grp, h = divmod(c, N_HEADS)
        e[c, grp * GROUP_W + h * HEAD_DIM: grp * GROUP_W + (h + 1) * HEAD_DIM] = 1.0
    return jnp.asarray(e, BF16)


def _lanes_per_head(v):
    return jnp.repeat(v, HEAD_DIM, axis=-1)


def _block_diag_heads(w):
    out = jnp.zeros((GROUP_W, GROUP_W), F32)
    for h in range(N_HEADS):
        out = out.at[h * HEAD_DIM:(h + 1) * HEAD_DIM, h * HEAD_DIM:(h + 1) * HEAD_DIM].set(w[h])
    return out


def _rope_tables(tc, tx):
    t = np.arange(tx)
    rows = (t // GRID_W).astype(np.float32)
    cols = (t % GRID_W).astype(np.float32)
    nf = HEAD_DIM // 4
    inv = (ROPE_BASE ** (-np.arange(nf, dtype=np.float32) / nf)).astype(np.float32)
    ang = jnp.concatenate([jnp.asarray(rows)[:, None] * inv, jnp.asarray(cols)[:, None] * inv], -1)
    cos, sin = jnp.cos(ang), jnp.sin(ang)
    cos_h = jnp.concatenate([cos, cos], -1)
    sin_h = jnp.concatenate([-sin, sin], -1)
    cos_f = jnp.tile(cos_h, (1, N_HEADS))
    sin_f = jnp.tile(sin_h, (1, N_HEADS))
    cos_f = jnp.concatenate([jnp.ones((tc, GROUP_W), F32), cos_f], 0)
    sin_f = jnp.concatenate([jnp.zeros((tc, GROUP_W), F32), sin_f], 0)
    return jnp.concatenate([cos_f, sin_f], -1)


def _route(idx8, n_tok):
    tk = n_tok * TOP_K
    flat_e = idx8.reshape(tk)
    order = jnp.argsort(flat_e)
    se = flat_e[order]
    counts = jnp.bincount(flat_e, length=N_EXPERTS)
    nblk = (counts + MOE_BLOCK - 1) // MOE_BLOCK
    blk_end = jnp.cumsum(nblk)
    pad_start = (blk_end - nblk) * MOE_BLOCK
    grp_start = jnp.cumsum(counts) - counts
    dest = (pad_start[se] + jnp.arange(tk) - grp_start[se]).astype(I32)
    nb = tk // MOE_BLOCK + N_EXPERTS
    slot_tok = jnp.zeros((nb * MOE_BLOCK,), I32).at[dest].set((order // TOP_K).astype(I32))
    pos = jnp.zeros((tk,), I32).at[order].set(dest)
    blk_e = jnp.minimum(jnp.searchsorted(blk_end, jnp.arange(nb), side="right"), N_EXPERTS - 1).astype(I32)
    nused = blk_end[-1:].astype(I32)
    return slot_tok.reshape(nb, 1, MOE_BLOCK), pos, blk_e, nused


def kernel(x, c, ctx, c_ctx, ada_w, ada_b, norm_mix, norm_ffn, norm_final, w_in, w_vres, w_out, rw_mu, rw_w0, rw_w2, rw_a0, rw_a2, rw_g2, rw_kk, rw_ka, rw_rk, rw_ln_w, rw_ln_b, rw_v0, rw_v2, ret_lambda, gdn_conv_w, gdn_a_log, gdn_dt_bias, gdn_norm_w, lru_conv_w, lru_conv_b, lru_w_a, lru_b_a, lru_w_x, lru_b_x, lru_lambda, router_w, router_bias, exp_w_gate, exp_w_up, exp_w_down, sh_w_gate, sh_w_up, sh_w_down):
    B, tx, _ = x.shape
    tc = ctx.shape[1]
    depth = w_in.shape[0]
    assert tc % TM == 0 and tx % TM == 0 and x.shape[2] == D_MODEL
    nc = tc // TM
    ncc = tc // CH
    S = tc + tx
    xs = jnp.concatenate([ctx, x], axis=1)
    cvec = jnp.concatenate([c, c_ctx[None, :], jnp.zeros((8 - B - 1, D_MODEL), F32)], 0)
    rope = _rope_tables(tc, tx)
    expand = _head_expand()
    vfirst = None
    for l in range(depth):
        ada = _ada_call(cvec, ada_w[l], ada_b[l][None, :]).reshape(8, 6, D_MODEL)
        mod = jnp.stack([jnp.broadcast_to(ada[B][None], (B, 6, D_MODEL)), ada[:B]], axis=1)

        w_l = _pack_w_in(w_in[l], None if l == 0 else w_vres[l - 1])
        p_rw, p_ret, p_gdn, p_lru, p_ab = _inproj_call(xs, norm_mix[l][None, :], mod, w_l, nc)

        rprm = _rwkv_params(l, rw_mu, rw_w0, rw_w2, rw_a0, rw_a2, rw_g2, rw_kk, rw_ka, rw_rk, rw_v0, rw_v2)
        sh_rw, dp_rw, bonus, g_rw = _rwkv_prep_call(p_rw, vfirst, rprm, nc)
        if l == 0:
            vfirst = sh_rw
        ys = _rwkv_scan_call(sh_rw, dp_rw, ncc)

        ros = _ret_call(p_ret, rope, _lanes_per_head(-ret_lambda[l]), ncc)

        gprm = {"conv_w": gdn_conv_w[l],
                "alog": jnp.pad(gdn_a_log[l].reshape(1, 2 * N_HEADS), ((0, 0), (0, 128 - 2 * N_HEADS))),
                "dtb": jnp.pad(gdn_dt_bias[l].reshape(1, 2 * N_HEADS), ((0, 0), (0, 128 - 2 * N_HEADS))),
                "expand": expand}
        sh_g, dp_g = _gdn_prep_call(p_gdn, p_ab, gprm, nc)
        gos = _gdn_scan_call(sh_g, dp_g, ncc)

        lprm = {"conv_w": lru_conv_w[l], "conv_b": lru_conv_b[l][None, :],
                "wa": jnp.stack([_block_diag_heads(lru_w_a[l, d]) for d in range(2)]).astype(BF16),
                "ba": lru_b_a[l],
                "wx": jnp.stack([_block_diag_heads(lru_w_x[l, d]) for d in range(2)]).astype(BF16),
                "bx": lru_b_x[l], "lam": lru_lambda[l]}
        lhs = _lru_call(p_lru, lprm, nc)

        xs = _mix_out_call(ys, bonus, g_rw, ros, p_ret, gos, p_gdn, lhs, p_lru, xs, mod, w_out[l].astype(BF16),
                           rw_ln_w[l][None, :], rw_ln_b[l][None, :], jnp.tile(gdn_norm_w[l], N_HEADS)[None, :], nc)

        rw_pad = jnp.pad(router_w[l], ((0, 0), (0, 128 - N_EXPERTS)))
        rb_pad = jnp.concatenate([router_bias[l], jnp.full((128 - N_EXPERTS,), -jnp.inf, F32)])[None, :]
        h2, idx, gate, shared = _moe_pre_call(xs, norm_ffn[l][None, :], mod, rw_pad, rb_pad,
                                              sh_w_gate[l].astype(BF16), sh_w_up[l].astype(BF16),
                                              sh_w_down[l].astype(BF16), nc)
        n_tok = B * S
        slot_tok, pos, blk_e, nused = _route(idx[:, :, :TOP_K], n_tok)
        y = _expert_call(blk_e, nused, slot_tok, h2.reshape(n_tok, D_MODEL), exp_w_gate[l].astype(BF16),
                         exp_w_up[l].astype(BF16), exp_w_down[l].astype(BF16))
        xs = _combine_call(pos.reshape(n_tok // COMB_T, 1, COMB_T * TOP_K), y, gate, shared, xs, mod, nc)
    return _final_norm_call(xs, norm_final[None, :], nc)
```

```python
import functools

import jax
import jax.numpy as jnp
import numpy as np
from jax import lax
from jax.experimental import pallas as pl
from jax.experimental.pallas import tpu as pltpu

F32, BF16, I32 = jnp.float32, jnp.bfloat16, jnp.int32

D_MODEL = 1024
GROUP_W = 256
HEAD_DIM = 64
N_HEADS = GROUP_W // HEAD_DIM
HEAD_SHIFT = 6
GRID_W = 64
ROPE_BASE = 10000.0
RMS_EPS = 1e-6
GN_EPS = 64e-5
LRU_C = 8.0
N_EXPERTS = 64
TOP_K = 8
EXPERT_FF = 256
ROUTED_SCALE = 2.5
MOE_BLOCK = 128
LORA_W = 32

TM = 256
CH = 64
SCAN_B = 4
HALO = 8
COMB_T = 64
VMEM_LIMIT_BYTES = 48 * 1024 * 1024

NN = (((1,), (0,)), ((), ()))
NT = (((1,), (1,)), ((), ()))
TN = (((0,), (0,)), ((), ()))
BNN = (((2,), (1,)), ((0,), (0,)))
BNT = (((2,), (2,)), ((0,), (0,)))
BTN = (((1,), (1,)), ((0,), (0,)))


def _cparams(*sem):
    return pltpu.CompilerParams(dimension_semantics=sem, vmem_limit_bytes=VMEM_LIMIT_BYTES)


def _mm(a, b, dims=NN):
    return lax.dot_general(a, b, dims, preferred_element_type=F32)


def _mmb(a, b, dims=NN):
    return _mm(a.astype(BF16), b.astype(BF16), dims)


def _split2(a):
    hi = a.astype(BF16)
    return hi, (a - hi.astype(F32)).astype(BF16)


def _split3(a):
    hi = a.astype(BF16)
    r = a - hi.astype(F32)
    mid = r.astype(BF16)
    return hi, mid, (r - mid.astype(F32)).astype(BF16)


def _mm3(a, b, dims=NN):
    ah, al = _split2(a)
    bh, bl = _split2(b)
    return _mm(ah, bh, dims) + (_mm(ah, bl, dims) + _mm(al, bh, dims))


def _mmx(a, b_exact, dims=NN):
    ah, am, al = _split3(a)
    return _mm(ah, b_exact, dims) + (_mm(am, b_exact, dims) + _mm(al, b_exact, dims))


def _xmm(a_exact, b, dims=NN):
    bh, bm, bl = _split3(b)
    return _mm(a_exact, bh, dims) + (_mm(a_exact, bm, dims) + _mm(a_exact, bl, dims))


def _iota(shape, dim):
    return lax.broadcasted_iota(I32, shape, dim)


def _head_mask(rows, cols):
    return (_iota((rows, cols), 0) >> HEAD_SHIFT) == (_iota((rows, cols), 1) >> HEAD_SHIFT)


def _head_mask_bf(rows, cols):
    return jnp.where(_head_mask(rows, cols), 1.0, 0.0).astype(BF16)


def _headsum(x, mbd_bf):
    return _mmx(x, mbd_bf)


def _bd(x, mask_bf):
    return jnp.concatenate([x.astype(BF16)] * N_HEADS, axis=1) * mask_bf


def _tri_inverse(a_all, mask_bf):
    def times(x, yh, yl):
        n = x.shape[1]
        xh, xl = _split2(x)
        top = _mm(jnp.concatenate([xh, xl], axis=1), yh, BNN)
        return (top[:, 0:n] + top[:, n:2 * n]) + _mm(xh, yl, BNN)

    t = _iota(a_all.shape, 1)
    j = _iota(a_all.shape, 2) & (CH - 1)
    p = jnp.where(t == j, 1.0, 0.0) + a_all
    ah, al = _split2(a_all)
    a = times(a_all, _bd(ah, mask_bf), _bd(al, mask_bf))
    levels = CH.bit_length() - 2
    for lvl in range(levels):
        ah, al = _split2(a)
        yh, yl = _bd(ah, mask_bf), _bd(al, mask_bf)
        if lvl + 1 < levels:
            both = times(jnp.concatenate([a, p], axis=1), yh, yl)
            a = both[:, 0:CH]
            p = p + both[:, CH:2 * CH]
        else:
            p = p + times(p, yh, yl)
    return p


def _scan_dist(ncols):
    shape = (2 * SCAN_B, CH, ncols)
    t = _iota(shape, 1)
    j = _iota(shape, 2) & (CH - 1)
    return jnp.where(_iota(shape, 0) >= SCAN_B, j - t, t - j)


def _tri_ones():
    return jnp.where(_scan_dist(CH) >= 0, 1.0, 0.0).astype(BF16)


def _chains(fwd_ref, rev_ref):
    return jnp.concatenate([fwd_ref[...], rev_ref[...]], axis=0)


def _rms_modulate(x, g, shift, scale):
    y = x * lax.rsqrt(jnp.mean(x * x, axis=-1, keepdims=True) + RMS_EPS) * g
    return y * (1.0 + scale) + shift


def _softplus(x):
    return jnp.maximum(x, 0.0) + jnp.log(1.0 + jnp.exp(-jnp.abs(x)))


def _silu(x):
    return x * jax.nn.sigmoid(x)


def _gelu_tanh(x):
    return 0.5 * x * (1.0 + jnp.tanh(np.sqrt(2.0 / np.pi).astype(np.float32) * (x + 0.044715 * (x * x * x))))


def _shift_rows(p, prev_row, next_row):
    n = p.shape[0]
    r = _iota((n, 1), 0)
    up = jnp.where(r == 0, prev_row, pltpu.roll(p, 1, 0))
    dn = jnp.where(r == n - 1, next_row, pltpu.roll(p, n - 1, 0))
    return up, dn


def _seq_edges(i, n_ctx_tiles, n_tiles):
    first = jnp.logical_or(i == 0, i == n_ctx_tiles)
    last = jnp.logical_or(i == n_ctx_tiles - 1, i == n_tiles - 1)
    return first, last


def _fill_conv_scratch(scr, cur, prev8, next8, first, last):
    n = cur.shape[0]
    scr[0:HALO, :] = jnp.where(first, 0.0, prev8)
    scr[HALO:HALO + n, :] = cur
    scr[HALO + n:2 * HALO + n, :] = jnp.where(last, 0.0, next8)


def _conv4(scr, w, n):
    return (scr[HALO - 2:HALO - 2 + n, :] * w[0:1, :] + scr[HALO - 1:HALO - 1 + n, :] * w[1:2, :]
            + scr[HALO:HALO + n, :] * w[2:3, :] + scr[HALO + 1:HALO + 1 + n, :] * w[3:4, :])


def _ada_kernel(c_ref, w_ref, b_ref, o_ref):
    c = c_ref[...]
    o_ref[...] = _mm3(_silu(c), w_ref[...]) + b_ref[...]


def _ada_call(cvec, w, b):
    n = w.shape[1]
    tn = 768
    return pl.pallas_call(
        _ada_kernel,
        grid=(n // tn,),
        in_specs=[pl.BlockSpec((8, D_MODEL), lambda j: (0, 0)),
                  pl.BlockSpec((D_MODEL, tn), lambda j: (0, j)),
                  pl.BlockSpec((1, tn), lambda j: (0, j))],
        out_specs=pl.BlockSpec((8, tn), lambda j: (0, j)),
        out_shape=jax.ShapeDtypeStruct((8, n), F32),
        compiler_params=_cparams("arbitrary"),
        name="ada_mod",
    )(cvec, w, b)


IN_COLS = (1024, 1024, 1024, 512, 128)


def _inproj_kernel(x_ref, g_ref, mod_ref, w_ref, rw_ref, ret_ref, gdn_ref, lru_ref, ab_ref):
    h = _rms_modulate(x_ref[...], g_ref[...], mod_ref[0:1, :], mod_ref[1:2, :])
    p = _mm(h.astype(BF16), w_ref[...])
    o = 0
    for ref, n in zip((rw_ref, ret_ref, gdn_ref, lru_ref, ab_ref), IN_COLS):
        ref[...] = p[:, o:o + n]
        o += n


def _mod_spec(nc):
    return pl.BlockSpec((None, None, 6, D_MODEL), lambda b, i: (b, jnp.minimum(i // nc, 1), 0, 0))


def _inproj_call(xs, gain, mod, w, nc):
    B, S, _ = xs.shape
    n = w.shape[1]
    row = lambda width: pl.BlockSpec((None, TM, width), lambda b, i: (b, i, 0))
    return pl.pallas_call(
        _inproj_kernel,
        grid=(B, S // TM),
        in_specs=[row(D_MODEL), pl.BlockSpec((1, D_MODEL), lambda b, i: (0, 0)), _mod_spec(nc),
                  pl.BlockSpec((D_MODEL, n), lambda b, i: (0, 0))],
        out_specs=[row(c) for c in IN_COLS],
        out_shape=[jax.ShapeDtypeStruct((B, S, c), F32) for c in IN_COLS],
        compiler_params=_cparams("arbitrary", "arbitrary"),
        name="in_proj",
    )(xs, gain, mod, w)


def _rwkv_prep_kernel(nc, nt, has_vres, *refs):
    if has_vres:
        (p_ref, pp_ref, pn_ref, vf_ref, mu_ref, w0_ref, w2_ref, a0_ref, a2_ref, g2_ref, kkw_ref, ka_ref, rk_ref,
         v0_ref, v2_ref, sh_ref, dp_ref, bonus_ref, gate_ref) = refs
    else:
        (p_ref, pp_ref, pn_ref, mu_ref, w0_ref, w2_ref, a0_ref, a2_ref, g2_ref, kkw_ref, ka_ref, rk_ref,
         sh_ref, dp_ref, bonus_ref, gate_ref) = refs
    i = pl.program_id(1)
    first, last = _seq_edges(i, nc, nt)
    p = p_ref[...]
    prev_row = jnp.where(first, 0.0, pp_ref[HALO - 1:HALO, :])
    next_row = jnp.where(last, 0.0, pn_ref[0:1, :])
    up, dn = _shift_rows(p, prev_row, next_row)
    ps = p + (0.5 * (up + dn) - p) * mu_ref[...]
    mbd = _head_mask(GROUP_W, GROUP_W).astype(BF16)

    r = ps[:, 0:256]
    k = ps[:, 256:512]
    v = ps[:, 512:768]
    x1 = ps[:, 768:896]
    x2 = ps[:, 896:1024]
    z = w0_ref[...] + _mmb(jnp.tanh(x1), w2_ref[...])
    lw = -np.exp(-0.5).astype(np.float32) * jax.nn.sigmoid(z)
    a = jax.nn.sigmoid(a0_ref[...] + _mmb(x1, a2_ref[...]))
    gate_ref[...] = _mmb(jax.nn.sigmoid(x2), g2_ref[...])
    kk = k * kkw_ref[...]
    kk = kk * lax.rsqrt(_headsum(kk * kk, mbd) + 1e-6)
    if has_vres:
        v = v + (vf_ref[...] - v) * jax.nn.sigmoid(v0_ref[...] + _mmb(x2, v2_ref[...]))
    sh_ref[:, 0:256] = r
    sh_ref[:, 256:512] = kk
    sh_ref[:, 512:768] = v
    ksum = jnp.zeros_like(k)
    for d in range(2):
        a_d = a[:, d * 256:(d + 1) * 256]
        kd = k * (1.0 + (a_d - 1.0) * ka_ref[...])
        ksum = ksum + kd
        dp_ref[d, :, 0:256] = lw[:, d * 256:(d + 1) * 256]
        dp_ref[d, :, 256:512] = kd
        dp_ref[d, :, 512:768] = kk * a_d
    bonus_ref[...] = _headsum(r * ksum * rk_ref[...], mbd) * v


def _halo_specs(width, nt):
    per = TM // HALO
    cur = pl.BlockSpec((None, TM, width), lambda b, i: (b, i, 0))
    prev = pl.BlockSpec((None, HALO, width), lambda b, i: (b, jnp.maximum(i * per - 1, 0), 0))
    nxt = pl.BlockSpec((None, HALO, width), lambda b, i: (b, jnp.minimum((i + 1) * per, nt * per - 1), 0))
    return cur, prev, nxt


def _const_spec(a):
    nd = a.ndim
    return pl.BlockSpec(a.shape, lambda b, i: (0,) * nd)


def _rwkv_prep_call(p_rw, vfirst_pack, prm, nc):
    B, S, _ = p_rw.shape
    nt = S // TM
    has_vres = vfirst_pack is not None
    cur, prev, nxt = _halo_specs(1024, nt)
    ins = [p_rw, p_rw, p_rw]
    specs = [cur, prev, nxt]
    if has_vres:
        ins.append(vfirst_pack)
        specs.append(pl.BlockSpec((None, TM, 256), lambda b, i: (b, i, 2)))
    names = ["mu", "w0", "w2", "a0", "a2", "g2", "kk", "ka", "rk"] + (["v0", "v2"] if has_vres else [])
    for nme in names:
        ins.append(prm[nme])
        specs.append(_const_spec(prm[nme]))
    row = lambda w: pl.BlockSpec((None, TM, w), lambda b, i: (b, i, 0))
    return pl.pallas_call(
        functools.partial(_rwkv_prep_kernel, nc, nt, has_vres),
        grid=(B, nt),
        in_specs=specs,
        out_specs=[row(768), pl.BlockSpec((None, 2, TM, 768), lambda b, i: (b, 0, i, 0)), row(256), row(256)],
        out_shape=[jax.ShapeDtypeStruct((B, S, 768), F32), jax.ShapeDtypeStruct((B, 2, S, 768), F32),
                   jax.ShapeDtypeStruct((B, S, 256), F32), jax.ShapeDtypeStruct((B, S, 256), F32)],
        compiler_params=_cparams("arbitrary", "arbitrary"),
        name="rwkv_prep",
    )(*ins)


def _rwkv_chains(sh, dp, s_ref, mask, mask_bf):
    r, kk, v = sh[:, :, 0:256], sh[:, :, 256:512], sh[:, :, 512:768]
    lw, kd, bb = dp[:, :, 0:256], dp[:, :, 256:512], dp[:, :, 512:768]
    g = _xmm(_tri_ones(), lw, BNN)
    g_tot = jnp.sum(lw, axis=1, keepdims=True)
    e_tot = jnp.exp(g_tot)
    eng = jnp.exp(-g)
    ab = -kk * jnp.exp(g - lw)
    bbar = bb * eng
    kbar = kd * eng
    rbar = r * jnp.exp(g)
    dist = _scan_dist(N_HEADS * CH)
    strict, incl = dist > 0, dist >= 0
    abb, rbb, vb = ab.astype(BF16), rbar.astype(BF16), v.astype(BF16)
    sc = _mm(jnp.concatenate([abb, rbb], axis=1),
             jnp.concatenate([_bd(bbar, mask_bf), _bd(kbar, mask_bf)], axis=1), BNT)
    a_ab = jnp.where(strict, sc[:, 0:CH, 0:256], 0.0)
    a_ak = jnp.where(strict, sc[:, 0:CH, 256:512], 0.0)
    a_rb = jnp.where(incl, sc[:, CH:2 * CH, 0:256], 0.0)
    a_rk = jnp.where(incl, sc[:, CH:2 * CH, 256:512], 0.0)
    tb = _tri_inverse(a_ab, mask_bf).astype(BF16)
    vbd = _bd(vb, mask_bf)
    rhs = _mm(a_ak.astype(BF16), vbd, BNN)
    wt = _mm(tb, _bd(abb, mask_bf), BNN)
    u0 = _mm(tb, _bd(rhs, mask_bf), BNN)
    y0 = _mm(a_rk.astype(BF16), vbd, BNN)
    s = s_ref[...]
    sb = s.astype(BF16)
    u = _mm(wt.astype(BF16), sb, BNT) + u0
    ub = u.astype(BF16)
    y = _mm(rbb, sb, BNT) + _mm(a_rb.astype(BF16), _bd(ub, mask_bf), BNN) + y0
    upd = _mm(jnp.concatenate([ub, vb], axis=1),
              jnp.concatenate([(bbar * e_tot).astype(BF16), (kbar * e_tot).astype(BF16)], axis=1), BTN)
    s_ref[...] = s * e_tot + jnp.where(mask, upd, 0.0)
    return y


def _rwkv_scan_kernel(sh0_ref, sh1_ref, dp0_ref, dp1_ref, y0_ref, y1_ref, s_ref):
    @pl.when(pl.program_id(1) == 0)
    def _():
        s_ref[...] = jnp.zeros_like(s_ref)

    y = _rwkv_chains(_chains(sh0_ref, sh1_ref), _chains(dp0_ref, dp1_ref), s_ref,
                     _head_mask(GROUP_W, GROUP_W), _head_mask_bf(GROUP_W, GROUP_W))
    y0_ref[...] = y[0:SCAN_B]
    y1_ref[...] = y[SCAN_B:2 * SCAN_B]


def _rev_idx(n, n_ctx, n_all):
    return jnp.where(n < n_ctx, n_ctx - 1 - n, n_all + n_ctx - 1 - n)


def _scan_specs(width_sh, width_dp, ncc, nch):
    fwd = lambda b, n: (b, n, 0)
    rev = lambda b, n: (b, _rev_idx(n, ncc, nch), 0)
    fwd_d = lambda b, n: (b, 0, n, 0)
    rev_d = lambda b, n: (b, 1, _rev_idx(n, ncc, nch), 0)
    return ([pl.BlockSpec((SCAN_B, CH, width_sh), fwd), pl.BlockSpec((SCAN_B, CH, width_sh), rev),
             pl.BlockSpec((SCAN_B, None, CH, width_dp), fwd_d), pl.BlockSpec((SCAN_B, None, CH, width_dp), rev_d)],
            [pl.BlockSpec((SCAN_B, CH, GROUP_W), fwd), pl.BlockSpec((SCAN_B, CH, GROUP_W), rev)])


def _state_scratch():
    return [pltpu.VMEM((2 * SCAN_B, GROUP_W, GROUP_W), F32)]


def _rwkv_scan_call(sh, dp, ncc):
    B, S, _ = sh.shape
    nch = S // CH
    in_specs, out_specs = _scan_specs(768, 768, ncc, nch)
    return pl.pallas_call(
        _rwkv_scan_kernel,
        grid=(B // SCAN_B, nch),
        in_specs=in_specs,
        out_specs=out_specs,
        out_shape=[jax.ShapeDtypeStruct((B, S, GROUP_W), F32)] * 2,
        scratch_shapes=_state_scratch(),
        compiler_params=_cparams("arbitrary", "arbitrary"),
        name="rwkv_scan",
    )(sh, sh, dp, dp)


def _per_dir(fwd, rev):
    f = jnp.broadcast_to(fwd[None], (SCAN_B,) + fwd.shape)
    r = jnp.broadcast_to(rev[None], (SCAN_B,) + rev.shape)
    return jnp.concatenate([f, r], axis=0)


def _ret_chains(p, cs, lg, r_ref, mask, mask_bf):
    nc = 2 * SCAN_B
    p2 = p.reshape(nc * CH, 1024)
    cs2 = cs.reshape(nc * CH, 512)
    lane = _iota((nc * CH, GROUP_W), 1)
    first_half = (lane & (HEAD_DIM - 1)) < HEAD_DIM // 2
    cos, sin = cs2[:, 0:256], cs2[:, 256:512]

    def rope(t):
        swapped = jnp.where(first_half, pltpu.roll(t, GROUP_W - HEAD_DIM // 2, 1), pltpu.roll(t, HEAD_DIM // 2, 1))
        return (t * cos + swapped * sin).reshape(nc, CH, GROUP_W)

    q = rope(p2[:, 0:256])
    k = rope(p2[:, 256:512]) * (HEAD_DIM ** -0.5)
    v = p[:, :, 512:768]
    dist = _scan_dist(N_HEADS * CH)
    decay = jnp.where(dist >= 0, jnp.exp(jnp.maximum(dist, 0).astype(F32) * lg), 0.0)
    shape = (nc, CH, GROUP_W)
    tpos = _iota(shape, 1)
    done = jnp.where(_iota(shape, 0) >= SCAN_B, CH - 1 - tpos, tpos)
    xi = jnp.exp((done + 1).astype(F32) * lg)
    zeta = jnp.exp((CH - 1 - done).astype(F32) * lg)
    g_c = jnp.exp(float(CH) * lg)
    vb = v.astype(BF16)
    s = _mm(q.astype(BF16), _bd(k, mask_bf), BNT) * decay
    rr = r_ref[...]
    o = _mm(s.astype(BF16), _bd(vb, mask_bf), BNN) + _mm((q * xi).astype(BF16), rr.astype(BF16), BNN)
    r_ref[...] = rr * g_c + jnp.where(mask, _mm((k * zeta).astype(BF16), vb, BTN), 0.0)
    return o


def _ret_kernel(p0_ref, p1_ref, cs0_ref, cs1_ref, lg_ref, o0_ref, o1_ref, r_ref):
    @pl.when(pl.program_id(1) == 0)
    def _():
        r_ref[...] = jnp.zeros_like(r_ref)

    o = _ret_chains(_chains(p0_ref, p1_ref), _per_dir(cs0_ref[...], cs1_ref[...]),
                    _per_dir(lg_ref[0:1, :], lg_ref[1:2, :]), r_ref,
                    _head_mask(GROUP_W, GROUP_W), _head_mask_bf(GROUP_W, GROUP_W))
    o0_ref[...] = o[0:SCAN_B]
    o1_ref[...] = o[SCAN_B:2 * SCAN_B]


def _ret_call(p_ret, cs, lg, ncc):
    B, S, _ = p_ret.shape
    nch = S // CH
    fwd = lambda b, n: (b, n, 0)
    rev = lambda b, n: (b, _rev_idx(n, ncc, nch), 0)
    fwd2 = lambda b, n: (n, 0)
    rev2 = lambda b, n: (_rev_idx(n, ncc, nch), 0)
    return pl.pallas_call(
        _ret_kernel,
        grid=(B // SCAN_B, nch),
        in_specs=[pl.BlockSpec((SCAN_B, CH, 1024), fwd), pl.BlockSpec((SCAN_B, CH, 1024), rev),
                  pl.BlockSpec((CH, 512), fwd2), pl.BlockSpec((CH, 512), rev2),
                  pl.BlockSpec((2, GROUP_W), lambda b, n: (0, 0))],
        out_specs=[pl.BlockSpec((SCAN_B, CH, GROUP_W), fwd), pl.BlockSpec((SCAN_B, CH, GROUP_W), rev)],
        out_shape=[jax.ShapeDtypeStruct((B, S, GROUP_W), F32)] * 2,
        scratch_shapes=_state_scratch(),
        compiler_params=_cparams("arbitrary", "arbitrary"),
        name="retention",
    )(p_ret, p_ret, cs, cs, lg)


def _gdn_prep_kernel(nc, nt, p_ref, pp_ref, pn_ref, ab_ref, cw_ref, alog_ref, dtb_ref, e_ref, sh_ref, dp_ref, scr):
    i = pl.program_id(1)
    first, last = _seq_edges(i, nc, nt)
    _fill_conv_scratch(scr, p_ref[:, 0:768], pp_ref[:, 0:768], pn_ref[:, 0:768], first, last)
    qkv = _silu(_conv4(scr, cw_ref[...], TM))
    mbd = _head_mask(GROUP_W, GROUP_W).astype(BF16)
    q, k = qkv[:, 0:256], qkv[:, 256:512]
    sh_ref[:, 0:256] = q * lax.rsqrt(_headsum(q * q, mbd) + 1e-6) * (HEAD_DIM ** -0.5)
    sh_ref[:, 256:512] = k * lax.rsqrt(_headsum(k * k, mbd) + 1e-6)
    sh_ref[:, 512:768] = qkv[:, 512:768]
    ab = ab_ref[...]
    ld = -jnp.exp(alog_ref[...]) * _softplus(ab + dtb_ref[...])
    vec = jnp.where(_iota(ab.shape, 1) < 2 * N_HEADS, ld, jax.nn.sigmoid(ab))
    ex = _mmx(vec, e_ref[...])
    for d in range(2):
        dp_ref[d, :, 0:256] = ex[:, d * 256:(d + 1) * 256]
        dp_ref[d, :, 256:512] = ex[:, 512 + d * 256:512 + (d + 1) * 256]


def _gdn_prep_call(p_gdn, p_ab, prm, nc):
    B, S, _ = p_gdn.shape
    nt = S // TM
    cur, prev, nxt = _halo_specs(1024, nt)
    consts = [prm["conv_w"], prm["alog"], prm["dtb"], prm["expand"]]
    row = lambda w: pl.BlockSpec((None, TM, w), lambda b, i: (b, i, 0))
    return pl.pallas_call(
        functools.partial(_gdn_prep_kernel, nc, nt),
        grid=(B, nt),
        in_specs=[cur, prev, nxt, row(128)] + [_const_spec(a) for a in consts],
        out_specs=[row(768), pl.BlockSpec((None, 2, TM, 512), lambda b, i: (b, 0, i, 0))],
        out_shape=[jax.ShapeDtypeStruct((B, S, 768), F32), jax.ShapeDtypeStruct((B, 2, S, 512), F32)],
        scratch_shapes=[pltpu.VMEM((TM + 2 * HALO, 768), F32)],
        compiler_params=_cparams("arbitrary", "arbitrary"),
        name="gdn_prep",
    )(p_gdn, p_gdn, p_gdn, p_ab, *consts)


def _gdn_chains(sh, dp, s_ref, mask, mask_bf):
    q, k, v = sh[:, :, 0:256], sh[:, :, 256:512], sh[:, :, 512:768]
    ld, beta = dp[:, :, 0:256], dp[:, :, 256:512]
    gc = _xmm(_tri_ones(), ld, BNN)
    g_tot = jnp.sum(ld, axis=1, keepdims=True)
    dist = _scan_dist(N_HEADS * CH)
    gc_cols = jnp.sum(jnp.where(dist == 0, gc, 0.0), axis=1, keepdims=True)
    gam = jnp.where(dist >= 0, jnp.exp(jnp.minimum(gc - gc_cols, 0.0)), 0.0)
    kb = k * beta
    kk = _mm(jnp.concatenate([kb, q], axis=1).astype(BF16), _bd(k, mask_bf), BNT)
    a_mat = jnp.where(dist > 0, kk[:, 0:CH] * gam, 0.0)
    attn = kk[:, CH:2 * CH] * gam
    tb = _tri_inverse(-a_mat, mask_bf).astype(BF16)
    egc = jnp.exp(gc)
    u = _mm(tb, _bd(v * beta, mask_bf), BNN)
    w = _mm(tb, _bd(kb * egc, mask_bf), BNN)
    s = s_ref[...]
    sb = s.astype(BF16)
    v_new = u - _mm(w.astype(BF16), sb, BNN)
    vnb = v_new.astype(BF16)
    o = _mm((q * egc).astype(BF16), sb, BNN) + _mm(attn.astype(BF16), _bd(vnb, mask_bf), BNN)
    upd = _mm((k * jnp.exp(g_tot - gc)).astype(BF16), vnb, BTN)
    s_ref[...] = s * jnp.exp(g_tot) + jnp.where(mask, upd, 0.0)
    return o


def _gdn_scan_kernel(sh0_ref, sh1_ref, dp0_ref, dp1_ref, o0_ref, o1_ref, s_ref):
    @pl.when(pl.program_id(1) == 0)
    def _():
        s_ref[...] = jnp.zeros_like(s_ref)

    o = _gdn_chains(_chains(sh0_ref, sh1_ref), _chains(dp0_ref, dp1_ref), s_ref,
                    _head_mask(GROUP_W, GROUP_W), _head_mask_bf(GROUP_W, GROUP_W))
    o0_ref[...] = o[0:SCAN_B]
    o1_ref[...] = o[SCAN_B:2 * SCAN_B]


def _gdn_scan_call(sh, dp, ncc):
    B, S, _ = sh.shape
    nch = S // CH
    in_specs, out_specs = _scan_specs(768, 512, ncc, nch)
    return pl.pallas_call(
        _gdn_scan_kernel,
        grid=(B // SCAN_B, nch),
        in_specs=in_specs,
        out_specs=out_specs,
        out_shape=[jax.ShapeDtypeStruct((B, S, GROUP_W), F32)] * 2,
        scratch_shapes=_state_scratch(),
        compiler_params=_cparams("arbitrary", "arbitrary"),
        name="gdn_scan",
    )(sh, sh, dp, dp)


def _lru_tile(d, rev, first, last, p_ref, pp_ref, pn_ref, cw_ref, cb_ref, wa_ref, ba_ref, wx_ref, bx_ref, lam_ref,
              h_ref, carry_ref, scr):
    _fill_conv_scratch(scr, p_ref[:, 0:256], pp_ref[:, 0:256], pn_ref[:, 0:256], first, last)
    xm = _conv4(scr, cw_ref[...], TM) + cb_ref[...]
    xb = xm.astype(BF16)
    r = jax.nn.sigmoid(_mm(xb, wa_ref[d]) + ba_ref[d:d + 1, :])
    ig = jax.nn.sigmoid(_mm(xb, wx_ref[d]) + bx_ref[d:d + 1, :])
    log_a = -LRU_C * _softplus(-lam_ref[d:d + 1, :]) * r
    a = jnp.exp(log_a)
    th = jnp.tanh(log_a)
    b = jnp.sqrt(-2.0 * th / (1.0 - th)) * ig * xm
    row = _iota((TM, 1), 0)
    s = 1
    while s < TM:
        if rev:
            ok = row < TM - s
            a_n, b_n = pltpu.roll(a, TM - s, 0), pltpu.roll(b, TM - s, 0)
        else:
            ok = row >= s
            a_n, b_n = pltpu.roll(a, s, 0), pltpu.roll(b, s, 0)
        b = jnp.where(ok, b + a * b_n, b)
        a = jnp.where(ok, a * a_n, a)
        s *= 2
    h = b + a * carry_ref[0:1, :]
    h_ref[...] = h
    end = h[0:1, :] if rev else h[TM - 1:TM, :]
    carry_ref[...] = jnp.broadcast_to(end, carry_ref.shape)


def _lru_kernel(nc, nt, p0, pp0, pn0, p1, pp1, pn1, cw, cb, wa, ba, wx, bx, lam, h0_ref, h1_ref, c0, c1, scr):
    i = pl.program_id(1)

    @pl.when(i == 0)
    def _():
        c0[...] = jnp.zeros_like(c0)
        c1[...] = jnp.zeros_like(c1)

    first, last = _seq_edges(i, nc, nt)
    _lru_tile(0, False, first, last, p0, pp0, pn0, cw, cb, wa, ba, wx, bx, lam, h0_ref, c0, scr)
    j = _rev_idx(i, nc, nt)
    first, last = _seq_edges(j, nc, nt)
    _lru_tile(1, True, first, last, p1, pp1, pn1, cw, cb, wa, ba, wx, bx, lam, h1_ref, c1, scr)


def _lru_call(p_lru, prm, nc):
    B, S, _ = p_lru.shape
    nt = S // TM
    per = TM // HALO
    rj = lambda i: _rev_idx(i, nc, nt)
    cur, prev, nxt = _halo_specs(512, nt)
    cur_r = pl.BlockSpec((None, TM, 512), lambda b, i: (b, rj(i), 0))
    prev_r = pl.BlockSpec((None, HALO, 512), lambda b, i: (b, jnp.maximum(rj(i) * per - 1, 0), 0))
    nxt_r = pl.BlockSpec((None, HALO, 512), lambda b, i: (b, jnp.minimum((rj(i) + 1) * per, nt * per - 1), 0))
    consts = [prm[k] for k in ("conv_w", "conv_b", "wa", "ba", "wx", "bx", "lam")]
    return pl.pallas_call(
        functools.partial(_lru_kernel, nc, nt),
        grid=(B, nt),
        in_specs=[cur, prev, nxt, cur_r, prev_r, nxt_r] + [_const_spec(a) for a in consts],
        out_specs=[pl.BlockSpec((None, TM, GROUP_W), lambda b, i: (b, i, 0)),
                   pl.BlockSpec((None, TM, GROUP_W), lambda b, i: (b, rj(i), 0))],
        out_shape=[jax.ShapeDtypeStruct((B, S, GROUP_W), F32)] * 2,
        scratch_shapes=[pltpu.VMEM((8, GROUP_W), F32), pltpu.VMEM((8, GROUP_W), F32),
                        pltpu.VMEM((TM + 2 * HALO, GROUP_W), F32)],
        compiler_params=_cparams("arbitrary", "arbitrary"),
        name="rglru",
    )(p_lru, p_lru, p_lru, p_lru, p_lru, p_lru, *consts)


def _mix_out_kernel(y0, y1, bonus, grw, ro0, ro1, gret, go0, go1, ggdn, lh0, lh1, glru, x_ref, mod_ref, w_ref,
                    lnw, lnb, gnw, o_ref):
    mbd = _head_mask(GROUP_W, GROUP_W).astype(BF16)
    inv = 1.0 / HEAD_DIM

    def head_rms(o):
        return o * lax.rsqrt(_headsum(o * o, mbd) * inv + RMS_EPS)

    y = y0[...] + y1[...]
    mu = _headsum(y, mbd) * inv
    dy = y - mu
    var = _headsum(dy * dy, mbd) * inv
    a = (dy * lax.rsqrt(var + GN_EPS) * lnw[...] + lnb[...] + bonus[...]) * grw[...]
    b = head_rms(ro0[...] + ro1[...]) * _silu(gret[...])
    g = head_rms(go0[...] + go1[...]) * gnw[...] * _silu(ggdn[...])
    r = (lh0[...] + lh1[...]) * _gelu_tanh(glru[...])
    w = w_ref[...]
    mix = (_mm(a.astype(BF16), w[0:256]) + _mm(b.astype(BF16), w[256:512])
           + _mm(g.astype(BF16), w[512:768]) + _mm(r.astype(BF16), w[768:1024]))
    o_ref[...] = x_ref[...] + mod_ref[2:3, :] * mix


def _mix_out_call(ys, bonus, grw, ros, p_ret, gos, p_gdn, lhs, p_lru, xs, mod, w_out, lnw, lnb, gnw, nc):
    B, S, _ = xs.shape
    row = lambda w, c=0: pl.BlockSpec((None, TM, w), lambda b, i: (b, i, c))
    g = row(GROUP_W)
    consts = [lnw, lnb, gnw]
    return pl.pallas_call(
        _mix_out_kernel,
        grid=(B, S // TM),
        in_specs=[g, g, g, g, g, g, row(GROUP_W, 3), g, g, row(GROUP_W, 3), g, g, row(GROUP_W, 1),
                  row(D_MODEL), _mod_spec(nc), pl.BlockSpec((D_MODEL, D_MODEL), lambda b, i: (0, 0))]
                 + [_const_spec(a) for a in consts],
        out_specs=row(D_MODEL),
        out_shape=jax.ShapeDtypeStruct((B, S, D_MODEL), F32),
        compiler_params=_cparams("arbitrary", "arbitrary"),
        name="mix_out",
    )(ys[0], ys[1], bonus, grw, ros[0], ros[1], p_ret, gos[0], gos[1], p_gdn, lhs[0], lhs[1], p_lru,
      xs, mod, w_out, *consts)


def _moe_pre_kernel(x_ref, g_ref, mod_ref, rw_ref, rb_ref, sg_ref, su_ref, sd_ref, h_ref, idx_ref, gate_ref, sh_ref):
    h = _rms_modulate(x_ref[...], g_ref[...], mod_ref[3:4, :], mod_ref[4:5, :])
    h_ref[...] = h
    scores = jax.nn.sigmoid(_mm3(h, rw_ref[...]))
    sel = scores + rb_ref[...]
    lane = _iota(scores.shape, 1)
    gates = jnp.zeros_like(scores)
    idxs = jnp.zeros(scores.shape, I32)
    total = jnp.zeros((scores.shape[0], 1), F32)
    for kk in range(TOP_K):
        m = jnp.max(sel, axis=-1, keepdims=True)
        idx = jnp.min(jnp.where(sel == m, lane, 2 * N_EXPERTS), axis=-1, keepdims=True)
        hit = lane == idx
        gk = jnp.sum(jnp.where(hit, scores, 0.0), axis=-1, keepdims=True)
        total = total + gk
        gates = jnp.where(lane == kk, gk, gates)
        idxs = jnp.where(lane == kk, idx, idxs)
        sel = jnp.where(hit, -jnp.inf, sel)
    gate_ref[...] = ROUTED_SCALE * gates / total
    idx_ref[...] = idxs
    hb = h.astype(BF16)
    act = _silu(_mm(hb, sg_ref[...])) * _mm(hb, su_ref[...])
    sh_ref[...] = _mm(act.astype(BF16), sd_ref[...])


def _moe_pre_call(xs, gain, mod, rw, rb, sg, su, sd, nc):
    B, S, _ = xs.shape
    row = lambda w: pl.BlockSpec((None, TM, w), lambda b, i: (b, i, 0))
    consts = [rw, rb, sg, su, sd]
    return pl.pallas_call(
        _moe_pre_kernel,
        grid=(B, S // TM),
        in_specs=[row(D_MODEL), pl.BlockSpec((1, D_MODEL), lambda b, i: (0, 0)), _mod_spec(nc)]
                 + [_const_spec(a) for a in consts],
        out_specs=[row(D_MODEL), row(128), row(128), row(D_MODEL)],
        out_shape=[jax.ShapeDtypeStruct((B, S, D_MODEL), F32), jax.ShapeDtypeStruct((B, S, 128), I32),
                   jax.ShapeDtypeStruct((B, S, 128), F32), jax.ShapeDtypeStruct((B, S, D_MODEL), F32)],
        compiler_params=_cparams("arbitrary", "arbitrary"),
        name="moe_pre",
    )(xs, gain, mod, *consts)


def _gather_rows(idx_ref, src_hbm, dst, sem):
    def start(r, c):
        pltpu.make_async_copy(src_hbm.at[pl.ds(idx_ref[0, 0, r], 1), :], dst.at[pl.ds(r, 1), :], sem).start()
        return c

    lax.fori_loop(0, MOE_BLOCK, start, 0, unroll=8)


def _scatter_rows(idx_ref, src, dst_hbm, sem):
    def start(r, c):
        pltpu.make_async_copy(src.at[pl.ds(r, 1), :], dst_hbm.at[pl.ds(idx_ref[0, 0, r], 1), :], sem).start()
        return c

    lax.fori_loop(0, MOE_BLOCK, start, 0, unroll=8)


def _wait_rows(hbm, vmem, sem, to_vmem):
    for r in range(MOE_BLOCK):
        a, b = hbm.at[pl.ds(0, 1), :], vmem.at[pl.ds(r, 1), :]
        (pltpu.make_async_copy(a, b, sem) if to_vmem else pltpu.make_async_copy(b, a, sem)).wait()


def _expert_kernel(blk_e_ref, tok_ref, tokn_ref, dst_ref, h_hbm, wg_ref, wu_ref, wd_ref, y_hbm, hb, yb, gsem, ssem):
    i = pl.program_id(0)
    nb = pl.num_programs(0)
    slot = lax.rem(i, 2)
    other = 1 - slot

    @pl.when(i == 0)
    def _():
        _gather_rows(tok_ref, h_hbm, hb.at[0], gsem.at[0])

    @pl.when(i + 1 < nb)
    def _():
        _gather_rows(tokn_ref, h_hbm, hb.at[other], gsem.at[other])

    _wait_rows(h_hbm, hb.at[slot], gsem.at[slot], True)

    @pl.when(i >= 2)
    def _():
        _wait_rows(y_hbm, yb.at[slot], ssem.at[slot], False)

    x = hb[slot].astype(BF16)
    act = _silu(_mm(x, wg_ref[...])) * _mm(x, wu_ref[...])
    yb[slot] = _mm(act.astype(BF16), wd_ref[...])
    _scatter_rows(dst_ref, yb.at[slot], y_hbm, ssem.at[slot])

    @pl.when(i == nb - 1)
    def _():
        _wait_rows(y_hbm, yb.at[other], ssem.at[other], False)
        _wait_rows(y_hbm, yb.at[slot], ssem.at[slot], False)


def _expert_call(blk_e, slot_tok, slot_dst, h_flat, wg, wu, wd, n_rows):
    nb = blk_e.shape[0]
    ff = wg.shape[2]
    idx_spec = lambda f: pl.BlockSpec((1, 1, MOE_BLOCK), f, memory_space=pltpu.SMEM)
    grid_spec = pltpu.PrefetchScalarGridSpec(
        num_scalar_prefetch=1,
        grid=(nb,),
        in_specs=[idx_spec(lambda i, be: (i, 0, 0)),
                  idx_spec(lambda i, be: (jnp.minimum(i + 1, nb - 1), 0, 0)),
                  idx_spec(lambda i, be: (i, 0, 0)),
                  pl.BlockSpec(memory_space=pl.ANY),
                  pl.BlockSpec((None, D_MODEL, ff), lambda i, be: (be[i], 0, 0)),
                  pl.BlockSpec((None, D_MODEL, ff), lambda i, be: (be[i], 0, 0)),
                  pl.BlockSpec((None, ff, D_MODEL), lambda i, be: (be[i], 0, 0))],
        out_specs=pl.BlockSpec(memory_space=pl.ANY),
        scratch_shapes=[pltpu.VMEM((2, MOE_BLOCK, D_MODEL), F32), pltpu.VMEM((2, MOE_BLOCK, D_MODEL), F32),
                        pltpu.SemaphoreType.DMA((2,)), pltpu.SemaphoreType.DMA((2,))],
    )
    return pl.pallas_call(
        _expert_kernel,
        grid_spec=grid_spec,
        out_shape=jax.ShapeDtypeStruct((n_rows, D_MODEL), F32),
        compiler_params=_cparams("arbitrary"),
        name="moe_experts",
    )(blk_e, slot_tok, slot_tok, slot_dst, h_flat, wg, wu, wd)


def _combine_kernel(y_ref, gate_ref, sh_ref, x_ref, mod_ref, o_ref):
    gate = gate_ref[...]
    routed = jnp.zeros((COMB_T, D_MODEL), F32)
    for k in range(TOP_K):
        routed = routed + gate[:, k:k + 1] * y_ref[k * COMB_T:(k + 1) * COMB_T, :]
    o_ref[...] = x_ref[...] + mod_ref[5:6, :] * (routed + sh_ref[...])


def _combine_call(y, gate, shared, xs, mod, nc):
    B, S, _ = xs.shape
    per = S // COMB_T
    row = lambda w: pl.BlockSpec((None, COMB_T, w), lambda b, i: (b, i, 0))
    mod_spec = pl.BlockSpec((None, None, 6, D_MODEL),
                            lambda b, i: (b, jnp.minimum(i // (nc * TM // COMB_T), 1), 0, 0))
    return pl.pallas_call(
        _combine_kernel,
        grid=(B, per),
        in_specs=[pl.BlockSpec((COMB_T * TOP_K, D_MODEL), lambda b, i: (b * per + i, 0)),
                  row(128), row(D_MODEL), row(D_MODEL), mod_spec],
        out_specs=row(D_MODEL),
        out_shape=jax.ShapeDtypeStruct((B, S, D_MODEL), F32),
        compiler_params=_cparams("arbitrary", "arbitrary"),
        name="moe_combine",
    )(y, gate, shared, xs, mod)


def _final_norm_kernel(x_ref, g_ref, o_ref):
    x = x_ref[...]
    o_ref[...] = x * lax.rsqrt(jnp.mean(x * x, axis=-1, keepdims=True) + RMS_EPS) * g_ref[...]


def _final_norm_call(xs, gain, nc):
    B, S, _ = xs.shape
    tx = S - nc * TM
    return pl.pallas_call(
        _final_norm_kernel,
        grid=(B, tx // TM),
        in_specs=[pl.BlockSpec((None, TM, D_MODEL), lambda b, i: (b, i + nc, 0)),
                  pl.BlockSpec((1, D_MODEL), lambda b, i: (0, 0))],
        out_specs=pl.BlockSpec((None, TM, D_MODEL), lambda b, i: (b, i, 0)),
        out_shape=jax.ShapeDtypeStruct((B, tx, D_MODEL), F32),
        compiler_params=_cparams("arbitrary", "arbitrary"),
        name="final_norm",
    )(xs, gain)


N_RWKV, N_RET, N_GDN, N_LRU = 960, 1024, 1040, 512


def _pack_w_in(w_in_l, w_vres_l):
    o_ret = N_RWKV
    o_gdn = o_ret + N_RET
    o_lru = o_gdn + N_GDN
    z = lambda n: jnp.zeros((D_MODEL, n), F32)
    vres = z(LORA_W) if w_vres_l is None else w_vres_l
    cols = [w_in_l[:, 0:N_RWKV], vres, z(1024 - N_RWKV - LORA_W),
            w_in_l[:, o_ret:o_gdn],
            w_in_l[:, o_gdn:o_gdn + 768], w_in_l[:, o_gdn + 784:o_lru],
            w_in_l[:, o_lru:o_lru + N_LRU],
            w_in_l[:, o_gdn + 768:o_gdn + 784], z(112)]
    return jnp.concatenate(cols, axis=1).astype(BF16)


def _block_diag2(w):
    r, c = w.shape[1], w.shape[2]
    z = jnp.zeros((r, c), F32)
    return jnp.concatenate([jnp.concatenate([w[0], z], 1), jnp.concatenate([z, w[1]], 1)], 0)


def _rwkv_params(l, rw_mu, rw_w0, rw_w2, rw_a0, rw_a2, rw_g2, rw_kk, rw_ka, rw_rk, rw_v0, rw_v2):
    z64 = jnp.zeros((2 * LORA_W, 2 * GROUP_W), F32)
    prm = {
        "mu": jnp.pad(rw_mu[l], (0, 1024 - N_RWKV))[None, :],
        "w0": rw_w0[l].reshape(1, 2 * GROUP_W),
        "w2": jnp.concatenate([_block_diag2(rw_w2[l]), z64], 0),
        "a0": rw_a0[l].reshape(1, 2 * GROUP_W),
        "a2": jnp.concatenate([z64, _block_diag2(rw_a2[l])], 0),
        "g2": jnp.concatenate([rw_g2[l], jnp.zeros((64, GROUP_W), F32)], 0),
        "kk": rw_kk[l][None, :],
        "ka": rw_ka[l][None, :],
        "rk": rw_rk[l].reshape(1, GROUP_W),
    }
    if l > 0:
        prm["v0"] = rw_v0[l - 1][None, :]
        prm["v2"] = jnp.concatenate([jnp.zeros((64, GROUP_W), F32), rw_v2[l - 1],
                                     jnp.zeros((128 - 64 - LORA_W, GROUP_W), F32)], 0)
    return prm


def _head_expand():
    e = np.zeros((128, 1024), np.float32)
    for c in range(4 * N_HEADS):
        grp, h = divmod(c, N_HEADS)
        e[c, grp * GROUP_W + h * HEAD_DIM: grp * GROUP_W + (h + 1) * HEAD_DIM] = 1.0
    return jnp.asarray(e, BF16)


def _lanes_per_head(v):
    return jnp.repeat(v, HEAD_DIM, axis=-1)


def _block_diag_heads(w):
    out = jnp.zeros((GROUP_W, GROUP_W), F32)
    for h in range(N_HEADS):
        out = out.at[h * HEAD_DIM:(h + 1) * HEAD_DIM, h * HEAD_DIM:(h + 1) * HEAD_DIM].set(w[h])
    return out


def _rope_tables(tc, tx):
    t = np.arange(tx)
    rows = (t // GRID_W).astype(np.float32)
    cols = (t % GRID_W).astype(np.float32)
    nf = HEAD_DIM // 4
    inv = (ROPE_BASE ** (-np.arange(nf, dtype=np.float32) / nf)).astype(np.float32)
    ang = jnp.concatenate([jnp.asarray(rows)[:, None] * inv, jnp.asarray(cols)[:, None] * inv], -1)
    cos, sin = jnp.cos(ang), jnp.sin(ang)
    cos_h = jnp.concatenate([cos, cos], -1)
    sin_h = jnp.concatenate([-sin, sin], -1)
    cos_f = jnp.tile(cos_h, (1, N_HEADS))
    sin_f = jnp.tile(sin_h, (1, N_HEADS))
    cos_f = jnp.concatenate([jnp.ones((tc, GROUP_W), F32), cos_f], 0)
    sin_f = jnp.concatenate([jnp.zeros((tc, GROUP_W), F32), sin_f], 0)
    return jnp.concatenate([cos_f, sin_f], -1)


def _route(idx8, n_tok):
    tk = n_tok * TOP_K
    flat_e = idx8.reshape(tk)
    order = jnp.argsort(flat_e).astype(I32)
    experts = jnp.arange(N_EXPERTS, dtype=I32)
    counts = jnp.sum((flat_e[:, None] == experts[None, :]).astype(I32), axis=0)
    nblk = (counts + MOE_BLOCK - 1) // MOE_BLOCK
    blk_end = jnp.cumsum(nblk)
    blk_start = blk_end - nblk
    grp_start = jnp.cumsum(counts) - counts
    nb = tk // MOE_BLOCK + N_EXPERTS
    bi = jnp.arange(nb, dtype=I32)
    blk_e = jnp.minimum(jnp.sum((blk_end[None, :] <= bi[:, None]).astype(I32), axis=1), N_EXPERTS - 1)
    j = bi - blk_start[blk_e]
    n_valid = jnp.clip(counts[blk_e] - j * MOE_BLOCK, 0, MOE_BLOCK)
    r = jnp.arange(MOE_BLOCK, dtype=I32)
    sorted_pos = (grp_start[blk_e] + j * MOE_BLOCK)[:, None] + r[None, :]
    valid = r[None, :] < n_valid[:, None]
    flat = order[jnp.clip(sorted_pos, 0, tk - 1)]
    tok = flat // TOP_K
    choice = flat - tok * TOP_K
    dst = (tok // COMB_T) * (COMB_T * TOP_K) + choice * COMB_T + tok % COMB_T
    spare = tk + (bi % 2)[:, None] * MOE_BLOCK + r[None, :]
    slot_dst = jnp.where(valid, dst, spare).astype(I32)
    slot_tok = jnp.where(valid, tok, 0).astype(I32)
    return (slot_tok.reshape(nb, 1, MOE_BLOCK), slot_dst.reshape(nb, 1, MOE_BLOCK), blk_e.astype(I32),
            tk + 2 * MOE_BLOCK)


def kernel(x, c, ctx, c_ctx, ada_w, ada_b, norm_mix, norm_ffn, norm_final, w_in, w_vres, w_out, rw_mu, rw_w0, rw_w2, rw_a0, rw_a2, rw_g2, rw_kk, rw_ka, rw_rk, rw_ln_w, rw_ln_b, rw_v0, rw_v2, ret_lambda, gdn_conv_w, gdn_a_log, gdn_dt_bias, gdn_norm_w, lru_conv_w, lru_conv_b, lru_w_a, lru_b_a, lru_w_x, lru_b_x, lru_lambda, router_w, router_bias, exp_w_gate, exp_w_up, exp_w_down, sh_w_gate, sh_w_up, sh_w_down):
    B, tx, _ = x.shape
    tc = ctx.shape[1]
    depth = w_in.shape[0]
    assert tc % TM == 0 and tx % TM == 0 and x.shape[2] == D_MODEL and B % SCAN_B == 0
    nc = tc // TM
    ncc = tc // CH
    S = tc + tx
    xs = jnp.concatenate([ctx, x], axis=1)
    cvec = jnp.concatenate([c, c_ctx[None, :], jnp.zeros((8 - B - 1, D_MODEL), F32)], 0)
    rope = _rope_tables(tc, tx)
    expand = _head_expand()
    vfirst = None
    for l in range(depth):
        ada = _ada_call(cvec, ada_w[l], ada_b[l][None, :]).reshape(8, 6, D_MODEL)
        mod = jnp.stack([jnp.broadcast_to(ada[B][None], (B, 6, D_MODEL)), ada[:B]], axis=1)

        w_l = _pack_w_in(w_in[l], None if l == 0 else w_vres[l - 1])
        p_rw, p_ret, p_gdn, p_lru, p_ab = _inproj_call(xs, norm_mix[l][None, :], mod, w_l, nc)

        rprm = _rwkv_params(l, rw_mu, rw_w0, rw_w2, rw_a0, rw_a2, rw_g2, rw_kk, rw_ka, rw_rk, rw_v0, rw_v2)
        sh_rw, dp_rw, bonus, g_rw = _rwkv_prep_call(p_rw, vfirst, rprm, nc)
        if l == 0:
            vfirst = sh_rw
        ys = _rwkv_scan_call(sh_rw, dp_rw, ncc)

        ros = _ret_call(p_ret, rope, _lanes_per_head(-ret_lambda[l]), ncc)

        gprm = {"conv_w": gdn_conv_w[l],
                "alog": jnp.pad(gdn_a_log[l].reshape(1, 2 * N_HEADS), ((0, 0), (0, 128 - 2 * N_HEADS))),
                "dtb": jnp.pad(gdn_dt_bias[l].reshape(1, 2 * N_HEADS), ((0, 0), (0, 128 - 2 * N_HEADS))),
                "expand": expand}
        sh_g, dp_g = _gdn_prep_call(p_gdn, p_ab, gprm, nc)
        gos = _gdn_scan_call(sh_g, dp_g, ncc)

        lprm = {"conv_w": lru_conv_w[l], "conv_b": lru_conv_b[l][None, :],
                "wa": jnp.stack([_block_diag_heads(lru_w_a[l, d]) for d in range(2)]).astype(BF16),
                "ba": lru_b_a[l],
                "wx": jnp.stack([_block_diag_heads(lru_w_x[l, d]) for d in range(2)]).astype(BF16),
                "bx": lru_b_x[l], "lam": lru_lambda[l]}
        lhs = _lru_call(p_lru, lprm, nc)

        xs = _mix_out_call(ys, bonus, g_rw, ros, p_ret, gos, p_gdn, lhs, p_lru, xs, mod, w_out[l].astype(BF16),
                           rw_ln_w[l][None, :], rw_ln_b[l][None, :], jnp.tile(gdn_norm_w[l], N_HEADS)[None, :], nc)

        rw_pad = jnp.pad(router_w[l], ((0, 0), (0, 128 - N_EXPERTS)))
        rb_pad = jnp.concatenate([router_bias[l], jnp.full((128 - N_EXPERTS,), -jnp.inf, F32)])[None, :]
        h2, idx, gate, shared = _moe_pre_call(xs, norm_ffn[l][None, :], mod, rw_pad, rb_pad,
                                              sh_w_gate[l].astype(BF16), sh_w_up[l].astype(BF16),
                                              sh_w_down[l].astype(BF16), nc)
        n_tok = B * S
        slot_tok, slot_dst, blk_e, n_rows = _route(idx[:, :, :TOP_K], n_tok)
        y = _expert_call(blk_e, slot_tok, slot_dst, h2.reshape(n_tok, D_MODEL), exp_w_gate[l].astype(BF16),
                         exp_w_up[l].astype(BF16), exp_w_down[l].astype(BF16), n_rows)
        xs = _combine_call(y, gate, shared, xs, mod, nc)
    return _final_norm_call(xs, norm_final[None, :], nc)
```

```python
import functools

import jax
import jax.numpy as jnp
import numpy as np
from jax import lax
from jax.experimental import pallas as pl
from jax.experimental.pallas import tpu as pltpu

F32, BF16, I32 = jnp.float32, jnp.bfloat16, jnp.int32

D_MODEL = 1024
GROUP_W = 256
HEAD_DIM = 64
N_HEADS = GROUP_W // HEAD_DIM
HEAD_SHIFT = 6
GRID_W = 64
ROPE_BASE = 10000.0
RMS_EPS = 1e-6
GN_EPS = 64e-5
LRU_C = 8.0
N_EXPERTS = 64
TOP_K = 8
EXPERT_FF = 256
ROUTED_SCALE = 2.5
MOE_BLOCK = 128
LORA_W = 32

TM = 256
CH = 64
SCAN_B = 4
HALO = 8
COMB_T = 64
VMEM_LIMIT_BYTES = 48 * 1024 * 1024
EXPERT_VMEM_LIMIT_BYTES = 56 * 1024 * 1024

NN = (((1,), (0,)), ((), ()))
NT = (((1,), (1,)), ((), ()))
TN = (((0,), (0,)), ((), ()))
BNN = (((2,), (1,)), ((0,), (0,)))
BNT = (((2,), (2,)), ((0,), (0,)))
BTN = (((1,), (1,)), ((0,), (0,)))


def _cparams(*sem):
    return pltpu.CompilerParams(dimension_semantics=sem, vmem_limit_bytes=VMEM_LIMIT_BYTES)


def _mm(a, b, dims=NN):
    return lax.dot_general(a, b, dims, preferred_element_type=F32)


def _mmb(a, b, dims=NN):
    return _mm(a.astype(BF16), b.astype(BF16), dims)


def _split2(a):
    hi = a.astype(BF16)
    return hi, (a - hi.astype(F32)).astype(BF16)


def _split3(a):
    hi = a.astype(BF16)
    r = a - hi.astype(F32)
    mid = r.astype(BF16)
    return hi, mid, (r - mid.astype(F32)).astype(BF16)


def _mm3(a, b, dims=NN):
    ah, al = _split2(a)
    bh, bl = _split2(b)
    return _mm(ah, bh, dims) + (_mm(ah, bl, dims) + _mm(al, bh, dims))


def _mmx(a, b_exact, dims=NN):
    ah, am, al = _split3(a)
    return _mm(ah, b_exact, dims) + (_mm(am, b_exact, dims) + _mm(al, b_exact, dims))


def _xmm(a_exact, b, dims=NN):
    bh, bm, bl = _split3(b)
    return _mm(a_exact, bh, dims) + (_mm(a_exact, bm, dims) + _mm(a_exact, bl, dims))


def _iota(shape, dim):
    return lax.broadcasted_iota(I32, shape, dim)


def _head_mask(rows, cols):
    return (_iota((rows, cols), 0) >> HEAD_SHIFT) == (_iota((rows, cols), 1) >> HEAD_SHIFT)


def _head_mask_bf(rows, cols):
    return jnp.where(_head_mask(rows, cols), 1.0, 0.0).astype(BF16)


def _headsum(x, mbd_bf):
    return _mmx(x, mbd_bf)


def _bd(x, mask_bf):
    return jnp.concatenate([x.astype(BF16)] * N_HEADS, axis=1) * mask_bf


def _tri_inverse(a_all, mask_bf):
    def times(x, yh, yl):
        n = x.shape[1]
        xh, xl = _split2(x)
        top = _mm(jnp.concatenate([xh, xl], axis=1), yh, BNN)
        return (top[:, 0:n] + top[:, n:2 * n]) + _mm(xh, yl, BNN)

    t = _iota(a_all.shape, 1)
    j = _iota(a_all.shape, 2) & (CH - 1)
    p = jnp.where(t == j, 1.0, 0.0) + a_all
    ah, al = _split2(a_all)
    a = times(a_all, _bd(ah, mask_bf), _bd(al, mask_bf))
    levels = CH.bit_length() - 2
    for lvl in range(levels):
        ah, al = _split2(a)
        yh, yl = _bd(ah, mask_bf), _bd(al, mask_bf)
        if lvl + 1 < levels:
            both = times(jnp.concatenate([a, p], axis=1), yh, yl)
            a = both[:, 0:CH]
            p = p + both[:, CH:2 * CH]
        else:
            p = p + times(p, yh, yl)
    return p


def _scan_dist(ncols):
    shape = (2 * SCAN_B, CH, ncols)
    t = _iota(shape, 1)
    j = _iota(shape, 2) & (CH - 1)
    return jnp.where(_iota(shape, 0) >= SCAN_B, j - t, t - j)


def _tri_ones():
    return jnp.where(_scan_dist(CH) >= 0, 1.0, 0.0).astype(BF16)


def _chains(fwd_ref, rev_ref):
    return jnp.concatenate([fwd_ref[...], rev_ref[...]], axis=0)


def _rms_modulate(x, g, shift, scale):
    y = x * lax.rsqrt(jnp.mean(x * x, axis=-1, keepdims=True) + RMS_EPS) * g
    return y * (1.0 + scale) + shift


def _softplus(x):
    return jnp.maximum(x, 0.0) + jnp.log(1.0 + jnp.exp(-jnp.abs(x)))


def _silu(x):
    return x * jax.nn.sigmoid(x)


def _gelu_tanh(x):
    return 0.5 * x * (1.0 + jnp.tanh(np.sqrt(2.0 / np.pi).astype(np.float32) * (x + 0.044715 * (x * x * x))))


def _shift_rows(p, prev_row, next_row):
    n = p.shape[0]
    r = _iota((n, 1), 0)
    up = jnp.where(r == 0, prev_row, pltpu.roll(p, 1, 0))
    dn = jnp.where(r == n - 1, next_row, pltpu.roll(p, n - 1, 0))
    return up, dn


def _seq_edges(i, n_ctx_tiles, n_tiles):
    first = jnp.logical_or(i == 0, i == n_ctx_tiles)
    last = jnp.logical_or(i == n_ctx_tiles - 1, i == n_tiles - 1)
    return first, last


def _fill_conv_scratch(scr, cur, prev8, next8, first, last):
    n = cur.shape[0]
    scr[0:HALO, :] = jnp.where(first, 0.0, prev8)
    scr[HALO:HALO + n, :] = cur
    scr[HALO + n:2 * HALO + n, :] = jnp.where(last, 0.0, next8)


def _conv4(scr, w, n):
    return (scr[HALO - 2:HALO - 2 + n, :] * w[0:1, :] + scr[HALO - 1:HALO - 1 + n, :] * w[1:2, :]
            + scr[HALO:HALO + n, :] * w[2:3, :] + scr[HALO + 1:HALO + 1 + n, :] * w[3:4, :])


def _ada_kernel(c_ref, w_ref, b_ref, o_ref):
    c = c_ref[...]
    o_ref[...] = _mm3(_silu(c), w_ref[...]) + b_ref[...]


def _ada_call(cvec, w, b):
    n = w.shape[1]
    tn = 768
    return pl.pallas_call(
        _ada_kernel,
        grid=(n // tn,),
        in_specs=[pl.BlockSpec((8, D_MODEL), lambda j: (0, 0)),
                  pl.BlockSpec((D_MODEL, tn), lambda j: (0, j)),
                  pl.BlockSpec((1, tn), lambda j: (0, j))],
        out_specs=pl.BlockSpec((8, tn), lambda j: (0, j)),
        out_shape=jax.ShapeDtypeStruct((8, n), F32),
        compiler_params=_cparams("arbitrary"),
        name="ada_mod",
    )(cvec, w, b)


IN_COLS = (1024, 1024, 1024, 512, 128)


def _inproj_kernel(x_ref, g_ref, mod_ref, w_ref, rw_ref, ret_ref, gdn_ref, lru_ref, ab_ref):
    h = _rms_modulate(x_ref[...], g_ref[...], mod_ref[0:1, :], mod_ref[1:2, :])
    p = _mm(h.astype(BF16), w_ref[...])
    o = 0
    for ref, n in zip((rw_ref, ret_ref, gdn_ref, lru_ref, ab_ref), IN_COLS):
        ref[...] = p[:, o:o + n]
        o += n


def _mod_spec(nc):
    return pl.BlockSpec((None, None, 6, D_MODEL), lambda b, i: (b, jnp.minimum(i // nc, 1), 0, 0))


def _inproj_call(xs, gain, mod, w, nc):
    B, S, _ = xs.shape
    n = w.shape[1]
    row = lambda width: pl.BlockSpec((None, TM, width), lambda b, i: (b, i, 0))
    return pl.pallas_call(
        _inproj_kernel,
        grid=(B, S // TM),
        in_specs=[row(D_MODEL), pl.BlockSpec((1, D_MODEL), lambda b, i: (0, 0)), _mod_spec(nc),
                  pl.BlockSpec((D_MODEL, n), lambda b, i: (0, 0))],
        out_specs=[row(c) for c in IN_COLS],
        out_shape=[jax.ShapeDtypeStruct((B, S, c), F32) for c in IN_COLS],
        compiler_params=_cparams("arbitrary", "arbitrary"),
        name="in_proj",
    )(xs, gain, mod, w)


def _rwkv_prep_kernel(nc, nt, has_vres, *refs):
    if has_vres:
        (p_ref, pp_ref, pn_ref, vf_ref, mu_ref, w0_ref, w2_ref, a0_ref, a2_ref, g2_ref, kkw_ref, ka_ref, rk_ref,
         v0_ref, v2_ref, sh_ref, dp_ref, bonus_ref, gate_ref) = refs
    else:
        (p_ref, pp_ref, pn_ref, mu_ref, w0_ref, w2_ref, a0_ref, a2_ref, g2_ref, kkw_ref, ka_ref, rk_ref,
         sh_ref, dp_ref, bonus_ref, gate_ref) = refs
    i = pl.program_id(1)
    first, last = _seq_edges(i, nc, nt)
    p = p_ref[...]
    prev_row = jnp.where(first, 0.0, pp_ref[HALO - 1:HALO, :])
    next_row = jnp.where(last, 0.0, pn_ref[0:1, :])
    up, dn = _shift_rows(p, prev_row, next_row)
    ps = p + (0.5 * (up + dn) - p) * mu_ref[...]
    mbd = _head_mask(GROUP_W, GROUP_W).astype(BF16)

    r = ps[:, 0:256]
    k = ps[:, 256:512]
    v = ps[:, 512:768]
    x1 = ps[:, 768:896]
    x2 = ps[:, 896:1024]
    z = w0_ref[...] + _mmb(jnp.tanh(x1), w2_ref[...])
    lw = -np.exp(-0.5).astype(np.float32) * jax.nn.sigmoid(z)
    a = jax.nn.sigmoid(a0_ref[...] + _mmb(x1, a2_ref[...]))
    gate_ref[...] = _mmb(jax.nn.sigmoid(x2), g2_ref[...])
    kk = k * kkw_ref[...]
    kk = kk * lax.rsqrt(_headsum(kk * kk, mbd) + 1e-6)
    if has_vres:
        v = v + (vf_ref[...] - v) * jax.nn.sigmoid(v0_ref[...] + _mmb(x2, v2_ref[...]))
    sh_ref[:, 0:256] = r
    sh_ref[:, 256:512] = kk
    sh_ref[:, 512:768] = v
    ksum = jnp.zeros_like(k)
    for d in range(2):
        a_d = a[:, d * 256:(d + 1) * 256]
        kd = k * (1.0 + (a_d - 1.0) * ka_ref[...])
        ksum = ksum + kd
        dp_ref[d, :, 0:256] = lw[:, d * 256:(d + 1) * 256]
        dp_ref[d, :, 256:512] = kd
        dp_ref[d, :, 512:768] = kk * a_d
    bonus_ref[...] = _headsum(r * ksum * rk_ref[...], mbd) * v


def _halo_specs(width, nt):
    per = TM // HALO
    cur = pl.BlockSpec((None, TM, width), lambda b, i: (b, i, 0))
    prev = pl.BlockSpec((None, HALO, width), lambda b, i: (b, jnp.maximum(i * per - 1, 0), 0))
    nxt = pl.BlockSpec((None, HALO, width), lambda b, i: (b, jnp.minimum((i + 1) * per, nt * per - 1), 0))
    return cur, prev, nxt


def _const_spec(a):
    nd = a.ndim
    return pl.BlockSpec(a.shape, lambda b, i: (0,) * nd)


def _rwkv_prep_call(p_rw, vfirst_pack, prm, nc):
    B, S, _ = p_rw.shape
    nt = S // TM
    has_vres = vfirst_pack is not None
    cur, prev, nxt = _halo_specs(1024, nt)
    ins = [p_rw, p_rw, p_rw]
    specs = [cur, prev, nxt]
    if has_vres:
        ins.append(vfirst_pack)
        specs.append(pl.BlockSpec((None, TM, 256), lambda b, i: (b, i, 2)))
    names = ["mu", "w0", "w2", "a0", "a2", "g2", "kk", "ka", "rk"] + (["v0", "v2"] if has_vres else [])
    for nme in names:
        ins.append(prm[nme])
        specs.append(_const_spec(prm[nme]))
    row = lambda w: pl.BlockSpec((None, TM, w), lambda b, i: (b, i, 0))
    return pl.pallas_call(
        functools.partial(_rwkv_prep_kernel, nc, nt, has_vres),
        grid=(B, nt),
        in_specs=specs,
        out_specs=[row(768), pl.BlockSpec((None, 2, TM, 768), lambda b, i: (b, 0, i, 0)), row(256), row(256)],
        out_shape=[jax.ShapeDtypeStruct((B, S, 768), F32), jax.ShapeDtypeStruct((B, 2, S, 768), F32),
                   jax.ShapeDtypeStruct((B, S, 256), F32), jax.ShapeDtypeStruct((B, S, 256), F32)],
        compiler_params=_cparams("arbitrary", "arbitrary"),
        name="rwkv_prep",
    )(*ins)


def _rwkv_chains(sh, dp, s_ref, mask, mask_bf):
    r, kk, v = sh[:, :, 0:256], sh[:, :, 256:512], sh[:, :, 512:768]
    lw, kd, bb = dp[:, :, 0:256], dp[:, :, 256:512], dp[:, :, 512:768]
    g = _xmm(_tri_ones(), lw, BNN)
    g_tot = jnp.sum(lw, axis=1, keepdims=True)
    e_tot = jnp.exp(g_tot)
    eng = jnp.exp(-g)
    ab = -kk * jnp.exp(g - lw)
    bbar = bb * eng
    kbar = kd * eng
    rbar = r * jnp.exp(g)
    dist = _scan_dist(N_HEADS * CH)
    strict, incl = dist > 0, dist >= 0
    abb, rbb, vb = ab.astype(BF16), rbar.astype(BF16), v.astype(BF16)
    sc = _mm(jnp.concatenate([abb, rbb], axis=1),
             jnp.concatenate([_bd(bbar, mask_bf), _bd(kbar, mask_bf)], axis=1), BNT)
    a_ab = jnp.where(strict, sc[:, 0:CH, 0:256], 0.0)
    a_ak = jnp.where(strict, sc[:, 0:CH, 256:512], 0.0)
    a_rb = jnp.where(incl, sc[:, CH:2 * CH, 0:256], 0.0)
    a_rk = jnp.where(incl, sc[:, CH:2 * CH, 256:512], 0.0)
    tb = _tri_inverse(a_ab, mask_bf).astype(BF16)
    vbd = _bd(vb, mask_bf)
    rhs = _mm(a_ak.astype(BF16), vbd, BNN)
    wt = _mm(tb, _bd(abb, mask_bf), BNN)
    u0 = _mm(tb, _bd(rhs, mask_bf), BNN)
    y0 = _mm(a_rk.astype(BF16), vbd, BNN)
    s = s_ref[...]
    sb = s.astype(BF16)
    u = _mm(wt.astype(BF16), sb, BNT) + u0
    ub = u.astype(BF16)
    y = _mm(rbb, sb, BNT) + _mm(a_rb.astype(BF16), _bd(ub, mask_bf), BNN) + y0
    upd = _mm(jnp.concatenate([ub, vb], axis=1),
              jnp.concatenate([(bbar * e_tot).astype(BF16), (kbar * e_tot).astype(BF16)], axis=1), BTN)
    s_ref[...] = s * e_tot + jnp.where(mask, upd, 0.0)
    return y


def _rwkv_scan_kernel(sh0_ref, sh1_ref, dp0_ref, dp1_ref, y0_ref, y1_ref, s_ref):
    @pl.when(pl.program_id(1) == 0)
    def _():
        s_ref[...] = jnp.zeros_like(s_ref)

    y = _rwkv_chains(_chains(sh0_ref, sh1_ref), _chains(dp0_ref, dp1_ref), s_ref,
                     _head_mask(GROUP_W, GROUP_W), _head_mask_bf(GROUP_W, GROUP_W))
    y0_ref[...] = y[0:SCAN_B]
    y1_ref[...] = y[SCAN_B:2 * SCAN_B]


def _rev_idx(n, n_ctx, n_all):
    return jnp.where(n < n_ctx, n_ctx - 1 - n, n_all + n_ctx - 1 - n)


def _scan_specs(width_sh, width_dp, ncc, nch):
    fwd = lambda b, n: (b, n, 0)
    rev = lambda b, n: (b, _rev_idx(n, ncc, nch), 0)
    fwd_d = lambda b, n: (b, 0, n, 0)
    rev_d = lambda b, n: (b, 1, _rev_idx(n, ncc, nch), 0)
    return ([pl.BlockSpec((SCAN_B, CH, width_sh), fwd), pl.BlockSpec((SCAN_B, CH, width_sh), rev),
             pl.BlockSpec((SCAN_B, None, CH, width_dp), fwd_d), pl.BlockSpec((SCAN_B, None, CH, width_dp), rev_d)],
            [pl.BlockSpec((SCAN_B, CH, GROUP_W), fwd), pl.BlockSpec((SCAN_B, CH, GROUP_W), rev)])


def _state_scratch():
    return [pltpu.VMEM((2 * SCAN_B, GROUP_W, GROUP_W), F32)]


def _rwkv_scan_call(sh, dp, ncc):
    B, S, _ = sh.shape
    nch = S // CH
    in_specs, out_specs = _scan_specs(768, 768, ncc, nch)
    return pl.pallas_call(
        _rwkv_scan_kernel,
        grid=(B // SCAN_B, nch),
        in_specs=in_specs,
        out_specs=out_specs,
        out_shape=[jax.ShapeDtypeStruct((B, S, GROUP_W), F32)] * 2,
        scratch_shapes=_state_scratch(),
        compiler_params=_cparams("arbitrary", "arbitrary"),
        name="rwkv_scan",
    )(sh, sh, dp, dp)


def _per_dir(fwd, rev):
    f = jnp.broadcast_to(fwd[None], (SCAN_B,) + fwd.shape)
    r = jnp.broadcast_to(rev[None], (SCAN_B,) + rev.shape)
    return jnp.concatenate([f, r], axis=0)


def _ret_chains(p, cs, lg, r_ref, mask, mask_bf):
    nc = 2 * SCAN_B
    p2 = p.reshape(nc * CH, 1024)
    cs2 = cs.reshape(nc * CH, 512)
    lane = _iota((nc * CH, GROUP_W), 1)
    first_half = (lane & (HEAD_DIM - 1)) < HEAD_DIM // 2
    cos, sin = cs2[:, 0:256], cs2[:, 256:512]

    def rope(t):
        swapped = jnp.where(first_half, pltpu.roll(t, GROUP_W - HEAD_DIM // 2, 1), pltpu.roll(t, HEAD_DIM // 2, 1))
        return (t * cos + swapped * sin).reshape(nc, CH, GROUP_W)

    q = rope(p2[:, 0:256])
    k = rope(p2[:, 256:512]) * (HEAD_DIM ** -0.5)
    v = p[:, :, 512:768]
    dist = _scan_dist(N_HEADS * CH)
    decay = jnp.where(dist >= 0, jnp.exp(jnp.maximum(dist, 0).astype(F32) * lg), 0.0)
    shape = (nc, CH, GROUP_W)
    tpos = _iota(shape, 1)
    done = jnp.where(_iota(shape, 0) >= SCAN_B, CH - 1 - tpos, tpos)
    xi = jnp.exp((done + 1).astype(F32) * lg)
    zeta = jnp.exp((CH - 1 - done).astype(F32) * lg)
    g_c = jnp.exp(float(CH) * lg)
    vb = v.astype(BF16)
    s = _mm(q.astype(BF16), _bd(k, mask_bf), BNT) * decay
    rr = r_ref[...]
    o = _mm(s.astype(BF16), _bd(vb, mask_bf), BNN) + _mm((q * xi).astype(BF16), rr.astype(BF16), BNN)
    r_ref[...] = rr * g_c + jnp.where(mask, _mm((k * zeta).astype(BF16), vb, BTN), 0.0)
    return o


def _ret_kernel(p0_ref, p1_ref, cs0_ref, cs1_ref, lg_ref, o0_ref, o1_ref, r_ref):
    @pl.when(pl.program_id(1) == 0)
    def _():
        r_ref[...] = jnp.zeros_like(r_ref)

    o = _ret_chains(_chains(p0_ref, p1_ref), _per_dir(cs0_ref[...], cs1_ref[...]),
                    _per_dir(lg_ref[0:1, :], lg_ref[1:2, :]), r_ref,
                    _head_mask(GROUP_W, GROUP_W), _head_mask_bf(GROUP_W, GROUP_W))
    o0_ref[...] = o[0:SCAN_B]
    o1_ref[...] = o[SCAN_B:2 * SCAN_B]


def _ret_call(p_ret, cs, lg, ncc):
    B, S, _ = p_ret.shape
    nch = S // CH
    fwd = lambda b, n: (b, n, 0)
    rev = lambda b, n: (b, _rev_idx(n, ncc, nch), 0)
    fwd2 = lambda b, n: (n, 0)
    rev2 = lambda b, n: (_rev_idx(n, ncc, nch), 0)
    return pl.pallas_call(
        _ret_kernel,
        grid=(B // SCAN_B, nch),
        in_specs=[pl.BlockSpec((SCAN_B, CH, 1024), fwd), pl.BlockSpec((SCAN_B, CH, 1024), rev),
                  pl.BlockSpec((CH, 512), fwd2), pl.BlockSpec((CH, 512), rev2),
                  pl.BlockSpec((2, GROUP_W), lambda b, n: (0, 0))],
        out_specs=[pl.BlockSpec((SCAN_B, CH, GROUP_W), fwd), pl.BlockSpec((SCAN_B, CH, GROUP_W), rev)],
        out_shape=[jax.ShapeDtypeStruct((B, S, GROUP_W), F32)] * 2,
        scratch_shapes=_state_scratch(),
        compiler_params=_cparams("arbitrary", "arbitrary"),
        name="retention",
    )(p_ret, p_ret, cs, cs, lg)


def _gdn_prep_kernel(nc, nt, p_ref, pp_ref, pn_ref, ab_ref, cw_ref, alog_ref, dtb_ref, e_ref, sh_ref, dp_ref, scr):
    i = pl.program_id(1)
    first, last = _seq_edges(i, nc, nt)
    _fill_conv_scratch(scr, p_ref[:, 0:768], pp_ref[:, 0:768], pn_ref[:, 0:768], first, last)
    qkv = _silu(_conv4(scr, cw_ref[...], TM))
    mbd = _head_mask(GROUP_W, GROUP_W).astype(BF16)
    q, k = qkv[:, 0:256], qkv[:, 256:512]
    sh_ref[:, 0:256] = q * lax.rsqrt(_headsum(q * q, mbd) + 1e-6) * (HEAD_DIM ** -0.5)
    sh_ref[:, 256:512] = k * lax.rsqrt(_headsum(k * k, mbd) + 1e-6)
    sh_ref[:, 512:768] = qkv[:, 512:768]
    ab = ab_ref[...]
    ld = -jnp.exp(alog_ref[...]) * _softplus(ab + dtb_ref[...])
    vec = jnp.where(_iota(ab.shape, 1) < 2 * N_HEADS, ld, jax.nn.sigmoid(ab))
    ex = _mmx(vec, e_ref[...])
    for d in range(2):
        dp_ref[d, :, 0:256] = ex[:, d * 256:(d + 1) * 256]
        dp_ref[d, :, 256:512] = ex[:, 512 + d * 256:512 + (d + 1) * 256]


def _gdn_prep_call(p_gdn, p_ab, prm, nc):
    B, S, _ = p_gdn.shape
    nt = S // TM
    cur, prev, nxt = _halo_specs(1024, nt)
    consts = [prm["conv_w"], prm["alog"], prm["dtb"], prm["expand"]]
    row = lambda w: pl.BlockSpec((None, TM, w), lambda b, i: (b, i, 0))
    return pl.pallas_call(
        functools.partial(_gdn_prep_kernel, nc, nt),
        grid=(B, nt),
        in_specs=[cur, prev, nxt, row(128)] + [_const_spec(a) for a in consts],
        out_specs=[row(768), pl.BlockSpec((None, 2, TM, 512), lambda b, i: (b, 0, i, 0))],
        out_shape=[jax.ShapeDtypeStruct((B, S, 768), F32), jax.ShapeDtypeStruct((B, 2, S, 512), F32)],
        scratch_shapes=[pltpu.VMEM((TM + 2 * HALO, 768), F32)],
        compiler_params=_cparams("arbitrary", "arbitrary"),
        name="gdn_prep",
    )(p_gdn, p_gdn, p_gdn, p_ab, *consts)


def _gdn_chains(sh, dp, s_ref, mask, mask_bf):
    q, k, v = sh[:, :, 0:256], sh[:, :, 256:512], sh[:, :, 512:768]
    ld, beta = dp[:, :, 0:256], dp[:, :, 256:512]
    gc = _xmm(_tri_ones(), ld, BNN)
    g_tot = jnp.sum(ld, axis=1, keepdims=True)
    dist = _scan_dist(N_HEADS * CH)
    gc_cols = jnp.sum(jnp.where(dist == 0, gc, 0.0), axis=1, keepdims=True)
    gam = jnp.where(dist >= 0, jnp.exp(jnp.minimum(gc - gc_cols, 0.0)), 0.0)
    kb = k * beta
    kk = _mm(jnp.concatenate([kb, q], axis=1).astype(BF16), _bd(k, mask_bf), BNT)
    a_mat = jnp.where(dist > 0, kk[:, 0:CH] * gam, 0.0)
    attn = kk[:, CH:2 * CH] * gam
    tb = _tri_inverse(-a_mat, mask_bf).astype(BF16)
    egc = jnp.exp(gc)
    u = _mm(tb, _bd(v * beta, mask_bf), BNN)
    w = _mm(tb, _bd(kb * egc, mask_bf), BNN)
    s = s_ref[...]
    sb = s.astype(BF16)
    v_new = u - _mm(w.astype(BF16), sb, BNN)
    vnb = v_new.astype(BF16)
    o = _mm((q * egc).astype(BF16), sb, BNN) + _mm(attn.astype(BF16), _bd(vnb, mask_bf), BNN)
    upd = _mm((k * jnp.exp(g_tot - gc)).astype(BF16), vnb, BTN)
    s_ref[...] = s * jnp.exp(g_tot) + jnp.where(mask, upd, 0.0)
    return o


def _gdn_scan_kernel(sh0_ref, sh1_ref, dp0_ref, dp1_ref, o0_ref, o1_ref, s_ref):
    @pl.when(pl.program_id(1) == 0)
    def _():
        s_ref[...] = jnp.zeros_like(s_ref)

    o = _gdn_chains(_chains(sh0_ref, sh1_ref), _chains(dp0_ref, dp1_ref), s_ref,
                    _head_mask(GROUP_W, GROUP_W), _head_mask_bf(GROUP_W, GROUP_W))
    o0_ref[...] = o[0:SCAN_B]
    o1_ref[...] = o[SCAN_B:2 * SCAN_B]


def _gdn_scan_call(sh, dp, ncc):
    B, S, _ = sh.shape
    nch = S // CH
    in_specs, out_specs = _scan_specs(768, 512, ncc, nch)
    return pl.pallas_call(
        _gdn_scan_kernel,
        grid=(B // SCAN_B, nch),
        in_specs=in_specs,
        out_specs=out_specs,
        out_shape=[jax.ShapeDtypeStruct((B, S, GROUP_W), F32)] * 2,
        scratch_shapes=_state_scratch(),
        compiler_params=_cparams("arbitrary", "arbitrary"),
        name="gdn_scan",
    )(sh, sh, dp, dp)


def _lru_tile(d, rev, first, last, p_ref, pp_ref, pn_ref, cw_ref, cb_ref, wa_ref, ba_ref, wx_ref, bx_ref, lam_ref,
              h_ref, carry_ref, scr):
    _fill_conv_scratch(scr, p_ref[:, 0:256], pp_ref[:, 0:256], pn_ref[:, 0:256], first, last)
    xm = _conv4(scr, cw_ref[...], TM) + cb_ref[...]
    xb = xm.astype(BF16)
    r = jax.nn.sigmoid(_mm(xb, wa_ref[d]) + ba_ref[d:d + 1, :])
    ig = jax.nn.sigmoid(_mm(xb, wx_ref[d]) + bx_ref[d:d + 1, :])
    log_a = -LRU_C * _softplus(-lam_ref[d:d + 1, :]) * r
    a = jnp.exp(log_a)
    th = jnp.tanh(log_a)
    b = jnp.sqrt(-2.0 * th / (1.0 - th)) * ig * xm
    row = _iota((TM, 1), 0)
    s = 1
    while s < TM:
        if rev:
            ok = row < TM - s
            a_n, b_n = pltpu.roll(a, TM - s, 0), pltpu.roll(b, TM - s, 0)
        else:
            ok = row >= s
            a_n, b_n = pltpu.roll(a, s, 0), pltpu.roll(b, s, 0)
        b = jnp.where(ok, b + a * b_n, b)
        a = jnp.where(ok, a * a_n, a)
        s *= 2
    h = b + a * carry_ref[0:1, :]
    h_ref[...] = h
    end = h[0:1, :] if rev else h[TM - 1:TM, :]
    carry_ref[...] = jnp.broadcast_to(end, carry_ref.shape)


def _lru_kernel(nc, nt, p0, pp0, pn0, p1, pp1, pn1, cw, cb, wa, ba, wx, bx, lam, h0_ref, h1_ref, c0, c1, scr):
    i = pl.program_id(1)

    @pl.when(i == 0)
    def _():
        c0[...] = jnp.zeros_like(c0)
        c1[...] = jnp.zeros_like(c1)

    first, last = _seq_edges(i, nc, nt)
    _lru_tile(0, False, first, last, p0, pp0, pn0, cw, cb, wa, ba, wx, bx, lam, h0_ref, c0, scr)
    j = _rev_idx(i, nc, nt)
    first, last = _seq_edges(j, nc, nt)
    _lru_tile(1, True, first, last, p1, pp1, pn1, cw, cb, wa, ba, wx, bx, lam, h1_ref, c1, scr)


def _lru_call(p_lru, prm, nc):
    B, S, _ = p_lru.shape
    nt = S // TM
    per = TM // HALO
    rj = lambda i: _rev_idx(i, nc, nt)
    cur, prev, nxt = _halo_specs(512, nt)
    cur_r = pl.BlockSpec((None, TM, 512), lambda b, i: (b, rj(i), 0))
    prev_r = pl.BlockSpec((None, HALO, 512), lambda b, i: (b, jnp.maximum(rj(i) * per - 1, 0), 0))
    nxt_r = pl.BlockSpec((None, HALO, 512), lambda b, i: (b, jnp.minimum((rj(i) + 1) * per, nt * per - 1), 0))
    consts = [prm[k] for k in ("conv_w", "conv_b", "wa", "ba", "wx", "bx", "lam")]
    return pl.pallas_call(
        functools.partial(_lru_kernel, nc, nt),
        grid=(B, nt),
        in_specs=[cur, prev, nxt, cur_r, prev_r, nxt_r] + [_const_spec(a) for a in consts],
        out_specs=[pl.BlockSpec((None, TM, GROUP_W), lambda b, i: (b, i, 0)),
                   pl.BlockSpec((None, TM, GROUP_W), lambda b, i: (b, rj(i), 0))],
        out_shape=[jax.ShapeDtypeStruct((B, S, GROUP_W), F32)] * 2,
        scratch_shapes=[pltpu.VMEM((8, GROUP_W), F32), pltpu.VMEM((8, GROUP_W), F32),
                        pltpu.VMEM((TM + 2 * HALO, GROUP_W), F32)],
        compiler_params=_cparams("arbitrary", "arbitrary"),
        name="rglru",
    )(p_lru, p_lru, p_lru, p_lru, p_lru, p_lru, *consts)


def _mix_out_kernel(y0, y1, bonus, grw, ro0, ro1, gret, go0, go1, ggdn, lh0, lh1, glru, x_ref, mod_ref, w_ref,
                    lnw, lnb, gnw, o_ref):
    mbd = _head_mask(GROUP_W, GROUP_W).astype(BF16)
    inv = 1.0 / HEAD_DIM

    def head_rms(o):
        return o * lax.rsqrt(_headsum(o * o, mbd) * inv + RMS_EPS)

    y = y0[...] + y1[...]
    mu = _headsum(y, mbd) * inv
    dy = y - mu
    var = _headsum(dy * dy, mbd) * inv
    a = (dy * lax.rsqrt(var + GN_EPS) * lnw[...] + lnb[...] + bonus[...]) * grw[...]
    b = head_rms(ro0[...] + ro1[...]) * _silu(gret[...])
    g = head_rms(go0[...] + go1[...]) * gnw[...] * _silu(ggdn[...])
    r = (lh0[...] + lh1[...]) * _gelu_tanh(glru[...])
    w = w_ref[...]
    mix = (_mm(a.astype(BF16), w[0:256]) + _mm(b.astype(BF16), w[256:512])
           + _mm(g.astype(BF16), w[512:768]) + _mm(r.astype(BF16), w[768:1024]))
    o_ref[...] = x_ref[...] + mod_ref[2:3, :] * mix


def _mix_out_call(ys, bonus, grw, ros, p_ret, gos, p_gdn, lhs, p_lru, xs, mod, w_out, lnw, lnb, gnw, nc):
    B, S, _ = xs.shape
    row = lambda w, c=0: pl.BlockSpec((None, TM, w), lambda b, i: (b, i, c))
    g = row(GROUP_W)
    consts = [lnw, lnb, gnw]
    return pl.pallas_call(
        _mix_out_kernel,
        grid=(B, S // TM),
        in_specs=[g, g, g, g, g, g, row(GROUP_W, 3), g, g, row(GROUP_W, 3), g, g, row(GROUP_W, 1),
                  row(D_MODEL), _mod_spec(nc), pl.BlockSpec((D_MODEL, D_MODEL), lambda b, i: (0, 0))]
                 + [_const_spec(a) for a in consts],
        out_specs=row(D_MODEL),
        out_shape=jax.ShapeDtypeStruct((B, S, D_MODEL), F32),
        compiler_params=_cparams("arbitrary", "arbitrary"),
        name="mix_out",
    )(ys[0], ys[1], bonus, grw, ros[0], ros[1], p_ret, gos[0], gos[1], p_gdn, lhs[0], lhs[1], p_lru,
      xs, mod, w_out, *consts)


def _moe_pre_kernel(x_ref, g_ref, mod_ref, rw_ref, rb_ref, sg_ref, su_ref, sd_ref, h_ref, idx_ref, gate_ref, sh_ref):
    h = _rms_modulate(x_ref[...], g_ref[...], mod_ref[3:4, :], mod_ref[4:5, :])
    hb = h.astype(BF16)
    bits = pltpu.bitcast(hb.astype(F32), jnp.uint32)
    half = D_MODEL // 2
    h_ref[...] = (bits[:, 0:half] >> 16) | (bits[:, half:D_MODEL] & jnp.uint32(0xFFFF0000))
    scores = jax.nn.sigmoid(_mm3(h, rw_ref[...]))
    sel = scores + rb_ref[...]
    lane = _iota(scores.shape, 1)
    gates = jnp.zeros_like(scores)
    idxs = jnp.zeros(scores.shape, I32)
    total = jnp.zeros((scores.shape[0], 1), F32)
    for kk in range(TOP_K):
        m = jnp.max(sel, axis=-1, keepdims=True)
        idx = jnp.min(jnp.where(sel == m, lane, 2 * N_EXPERTS), axis=-1, keepdims=True)
        hit = lane == idx
        gk = jnp.sum(jnp.where(hit, scores, 0.0), axis=-1, keepdims=True)
        total = total + gk
        gates = jnp.where(lane == kk, gk, gates)
        idxs = jnp.where(lane == kk, idx, idxs)
        sel = jnp.where(hit, -jnp.inf, sel)
    gate_ref[...] = ROUTED_SCALE * gates / total
    idx_ref[...] = idxs
    act = _silu(_mm(hb, sg_ref[...])) * _mm(hb, su_ref[...])
    sh_ref[...] = _mm(act.astype(BF16), sd_ref[...])


def _moe_pre_call(xs, gain, mod, rw, rb, sg, su, sd, nc):
    B, S, _ = xs.shape
    row = lambda w: pl.BlockSpec((None, TM, w), lambda b, i: (b, i, 0))
    consts = [rw, rb, sg, su, sd]
    return pl.pallas_call(
        _moe_pre_kernel,
        grid=(B, S // TM),
        in_specs=[row(D_MODEL), pl.BlockSpec((1, D_MODEL), lambda b, i: (0, 0)), _mod_spec(nc)]
                 + [_const_spec(a) for a in consts],
        out_specs=[row(D_MODEL // 2), row(128), row(128), row(D_MODEL)],
        out_shape=[jax.ShapeDtypeStruct((B, S, D_MODEL // 2), jnp.uint32), jax.ShapeDtypeStruct((B, S, 128), I32),
                   jax.ShapeDtypeStruct((B, S, 128), F32), jax.ShapeDtypeStruct((B, S, D_MODEL), F32)],
        compiler_params=_cparams("arbitrary", "arbitrary"),
        name="moe_pre",
    )(xs, gain, mod, *consts)


def _row_out_copy(src, dst_hbm, sem, r, dst_row):
    return pltpu.make_async_copy(src.at[pl.ds(r, 1), :], dst_hbm.at[pl.ds(dst_row, 1), :], sem)


def _wait_block_out(yb_slot, y_hbm, sem):
    for r in range(MOE_BLOCK):
        _row_out_copy(yb_slot, y_hbm, sem, r, 0).wait()


def _expert_kernel(blk_e_ref, tok_ref, dst_ref, hp_hbm, wg_ref, wu_ref, wd_ref, y_hbm, tab, xb, yb, tsem, ssem):
    i = pl.program_id(0)
    nb = pl.num_programs(0)
    slot = lax.rem(i, 2)
    other = 1 - slot

    @pl.when(i == 0)
    def _():
        load = pltpu.make_async_copy(hp_hbm, tab, tsem)
        load.start()
        load.wait()

    for r in range(MOE_BLOCK):
        xb[pl.ds(r, 1), :] = tab[pl.ds(tok_ref[0, 0, r], 1), :]
    u = xb[...]
    lo = pltpu.bitcast(u << 16, F32).astype(BF16)
    hi = pltpu.bitcast(u & jnp.uint32(0xFFFF0000), F32).astype(BF16)
    x = jnp.concatenate([lo, hi], axis=1)

    @pl.when(i >= 2)
    def _():
        _wait_block_out(yb.at[slot], y_hbm, ssem.at[slot])

    act = _silu(_mm(x, wg_ref[...])) * _mm(x, wu_ref[...])
    yb[slot] = _mm(act.astype(BF16), wd_ref[...])
    for r in range(MOE_BLOCK):
        _row_out_copy(yb.at[slot], y_hbm, ssem.at[slot], r, dst_ref[0, 0, r]).start()

    @pl.when(i == nb - 1)
    def _():
        _wait_block_out(yb.at[other], y_hbm, ssem.at[other])
        _wait_block_out(yb.at[slot], y_hbm, ssem.at[slot])


def _expert_call(blk_e, slot_tok, slot_dst, hp_flat, wg, wu, wd, n_rows):
    nb = blk_e.shape[0]
    ff = wg.shape[2]
    n_tok, half = hp_flat.shape
    idx_spec = pl.BlockSpec((1, 1, MOE_BLOCK), lambda i, be: (i, 0, 0), memory_space=pltpu.SMEM)
    grid_spec = pltpu.PrefetchScalarGridSpec(
        num_scalar_prefetch=1,
        grid=(nb,),
        in_specs=[idx_spec, idx_spec,
                  pl.BlockSpec(memory_space=pl.ANY),
                  pl.BlockSpec((None, D_MODEL, ff), lambda i, be: (be[i], 0, 0)),
                  pl.BlockSpec((None, D_MODEL, ff), lambda i, be: (be[i], 0, 0)),
                  pl.BlockSpec((None, ff, D_MODEL), lambda i, be: (be[i], 0, 0))],
        out_specs=pl.BlockSpec(memory_space=pl.ANY),
        scratch_shapes=[pltpu.VMEM((n_tok, half), jnp.uint32), pltpu.VMEM((MOE_BLOCK, half), jnp.uint32),
                        pltpu.VMEM((2, MOE_BLOCK, D_MODEL), F32),
                        pltpu.SemaphoreType.DMA(()), pltpu.SemaphoreType.DMA((2,))],
    )
    return pl.pallas_call(
        _expert_kernel,
        grid_spec=grid_spec,
        out_shape=jax.ShapeDtypeStruct((n_rows, D_MODEL), F32),
        compiler_params=pltpu.CompilerParams(dimension_semantics=("arbitrary",),
                                             vmem_limit_bytes=EXPERT_VMEM_LIMIT_BYTES),
        name="moe_experts",
    )(blk_e, slot_tok, slot_dst, hp_flat, wg, wu, wd)


def _combined(y_ref, gate_ref, sh_ref, x_ref, mod_ref):
    gate = gate_ref[...]
    routed = jnp.zeros((COMB_T, D_MODEL), F32)
    for k in range(TOP_K):
        routed = routed + gate[:, k:k + 1] * y_ref[k * COMB_T:(k + 1) * COMB_T, :]
    return x_ref[...] + mod_ref[5:6, :] * (routed + sh_ref[...])


def _combine_kernel(y_ref, gate_ref, sh_ref, x_ref, mod_ref, o_ref):
    o_ref[...] = _combined(y_ref, gate_ref, sh_ref, x_ref, mod_ref)


def _combine_norm_kernel(y_ref, gate_ref, sh_ref, x_ref, mod_ref, g_ref, o_ref):
    x = _combined(y_ref, gate_ref, sh_ref, x_ref, mod_ref)
    o_ref[...] = x * lax.rsqrt(jnp.mean(x * x, axis=-1, keepdims=True) + RMS_EPS) * g_ref[...]


def _combine_call(y, gate, shared, xs, mod, nc, final_gain=None):
    B, S, _ = xs.shape
    per = S // COMB_T
    skip = 0 if final_gain is None else nc * TM // COMB_T
    row = lambda w: pl.BlockSpec((None, COMB_T, w), lambda b, i: (b, i + skip, 0))
    mod_spec = pl.BlockSpec((None, None, 6, D_MODEL),
                            lambda b, i: (b, jnp.minimum((i + skip) // (nc * TM // COMB_T), 1), 0, 0))
    in_specs = [pl.BlockSpec((COMB_T * TOP_K, D_MODEL), lambda b, i: (b * per + i + skip, 0)),
                row(128), row(D_MODEL), row(D_MODEL), mod_spec]
    args = [y, gate, shared, xs, mod]
    if final_gain is not None:
        in_specs.append(pl.BlockSpec((1, D_MODEL), lambda b, i: (0, 0)))
        args.append(final_gain)
    return pl.pallas_call(
        _combine_kernel if final_gain is None else _combine_norm_kernel,
        grid=(B, per - skip),
        in_specs=in_specs,
        out_specs=pl.BlockSpec((None, COMB_T, D_MODEL), lambda b, i: (b, i, 0)),
        out_shape=jax.ShapeDtypeStruct((B, S - skip * COMB_T, D_MODEL), F32),
        compiler_params=_cparams("arbitrary", "arbitrary"),
        name="moe_combine",
    )(*args)


N_RWKV, N_RET, N_GDN, N_LRU = 960, 1024, 1040, 512


def _pack_w_in(w_in_l, w_vres_l):
    o_ret = N_RWKV
    o_gdn = o_ret + N_RET
    o_lru = o_gdn + N_GDN
    z = lambda n: jnp.zeros((D_MODEL, n), F32)
    vres = z(LORA_W) if w_vres_l is None else w_vres_l
    cols = [w_in_l[:, 0:N_RWKV], vres, z(1024 - N_RWKV - LORA_W),
            w_in_l[:, o_ret:o_gdn],
            w_in_l[:, o_gdn:o_gdn + 768], w_in_l[:, o_gdn + 784:o_lru],
            w_in_l[:, o_lru:o_lru + N_LRU],
            w_in_l[:, o_gdn + 768:o_gdn + 784], z(112)]
    return jnp.concatenate(cols, axis=1).astype(BF16)


def _block_diag2(w):
    r, c = w.shape[1], w.shape[2]
    z = jnp.zeros((r, c), F32)
    return jnp.concatenate([jnp.concatenate([w[0], z], 1), jnp.concatenate([z, w[1]], 1)], 0)


def _rwkv_params(l, rw_mu, rw_w0, rw_w2, rw_a0, rw_a2, rw_g2, rw_kk, rw_ka, rw_rk, rw_v0, rw_v2):
    z64 = jnp.zeros((2 * LORA_W, 2 * GROUP_W), F32)
    prm = {
        "mu": jnp.pad(rw_mu[l], (0, 1024 - N_RWKV))[None, :],
        "w0": rw_w0[l].reshape(1, 2 * GROUP_W),
        "w2": jnp.concatenate([_block_diag2(rw_w2[l]), z64], 0),
        "a0": rw_a0[l].reshape(1, 2 * GROUP_W),
        "a2": jnp.concatenate([z64, _block_diag2(rw_a2[l])], 0),
        "g2": jnp.concatenate([rw_g2[l], jnp.zeros((64, GROUP_W), F32)], 0),
        "kk": rw_kk[l][None, :],
        "ka": rw_ka[l][None, :],
        "rk": rw_rk[l].reshape(1, GROUP_W),
    }
    if l > 0:
        prm["v0"] = rw_v0[l - 1][None, :]
        prm["v2"] = jnp.concatenate([jnp.zeros((64, GROUP_W), F32), rw_v2[l - 1],
                                     jnp.zeros((128 - 64 - LORA_W, GROUP_W), F32)], 0)
    return prm


def _head_expand():
    e = np.zeros((128, 1024), np.float32)
    for c in range(4 * N_HEADS):
        grp, h = divmod(c, N_HEADS)
        e[c, grp * GROUP_W + h * HEAD_DIM: grp * GROUP_W + (h + 1) * HEAD_DIM] = 1.0
    return jnp.asarray(e, BF16)


def _lanes_per_head(v):
    return jnp.repeat(v, HEAD_DIM, axis=-1)


def _block_diag_heads(w):
    out = jnp.zeros((GROUP_W, GROUP_W), F32)
    for h in range(N_HEADS):
        out = out.at[h * HEAD_DIM:(h + 1) * HEAD_DIM, h * HEAD_DIM:(h + 1) * HEAD_DIM].set(w[h])
    return out


def _rope_tables(tc, tx):
    t = np.arange(tx)
    rows = (t // GRID_W).astype(np.float32)
    cols = (t % GRID_W).astype(np.float32)
    nf = HEAD_DIM // 4
    inv = (ROPE_BASE ** (-np.arange(nf, dtype=np.float32) / nf)).astype(np.float32)
    ang = jnp.concatenate([jnp.asarray(rows)[:, None] * inv, jnp.asarray(cols)[:, None] * inv], -1)
    cos, sin = jnp.cos(ang), jnp.sin(ang)
    cos_h = jnp.concatenate([cos, cos], -1)
    sin_h = jnp.concatenate([-sin, sin], -1)
    cos_f = jnp.tile(cos_h, (1, N_HEADS))
    sin_f = jnp.tile(sin_h, (1, N_HEADS))
    cos_f = jnp.concatenate([jnp.ones((tc, GROUP_W), F32), cos_f], 0)
    sin_f = jnp.concatenate([jnp.zeros((tc, GROUP_W), F32), sin_f], 0)
    return jnp.concatenate([cos_f, sin_f], -1)


def _route(idx8, n_tok):
    tk = n_tok * TOP_K
    flat_e = idx8.reshape(tk)
    order = jnp.argsort(flat_e).astype(I32)
    experts = jnp.arange(N_EXPERTS, dtype=I32)
    counts = jnp.sum((flat_e[:, None] == experts[None, :]).astype(I32), axis=0)
    nblk = (counts + MOE_BLOCK - 1) // MOE_BLOCK
    blk_end = jnp.cumsum(nblk)
    blk_start = blk_end - nblk
    grp_start = jnp.cumsum(counts) - counts
    nb = tk // MOE_BLOCK + N_EXPERTS
    bi = jnp.arange(nb, dtype=I32)
    blk_e = jnp.minimum(jnp.sum((blk_end[None, :] <= bi[:, None]).astype(I32), axis=1), N_EXPERTS - 1)
    j = bi - blk_start[blk_e]
    n_valid = jnp.clip(counts[blk_e] - j * MOE_BLOCK, 0, MOE_BLOCK)
    r = jnp.arange(MOE_BLOCK, dtype=I32)
    sorted_pos = (grp_start[blk_e] + j * MOE_BLOCK)[:, None] + r[None, :]
    valid = r[None, :] < n_valid[:, None]
    flat = order[jnp.clip(sorted_pos, 0, tk - 1)]
    tok = flat // TOP_K
    choice = flat - tok * TOP_K
    dst = (tok // COMB_T) * (COMB_T * TOP_K) + choice * COMB_T + tok % COMB_T
    spare = tk + (bi % 2)[:, None] * MOE_BLOCK + r[None, :]
    slot_dst = jnp.where(valid, dst, spare).astype(I32)
    slot_tok = jnp.where(valid, tok, 0).astype(I32)
    return (slot_tok.reshape(nb, 1, MOE_BLOCK), slot_dst.reshape(nb, 1, MOE_BLOCK), blk_e.astype(I32),
            tk + 2 * MOE_BLOCK)


def kernel(x, c, ctx, c_ctx, ada_w, ada_b, norm_mix, norm_ffn, norm_final, w_in, w_vres, w_out, rw_mu, rw_w0, rw_w2, rw_a0, rw_a2, rw_g2, rw_kk, rw_ka, rw_rk, rw_ln_w, rw_ln_b, rw_v0, rw_v2, ret_lambda, gdn_conv_w, gdn_a_log, gdn_dt_bias, gdn_norm_w, lru_conv_w, lru_conv_b, lru_w_a, lru_b_a, lru_w_x, lru_b_x, lru_lambda, router_w, router_bias, exp_w_gate, exp_w_up, exp_w_down, sh_w_gate, sh_w_up, sh_w_down):
    B, tx, _ = x.shape
    tc = ctx.shape[1]
    depth = w_in.shape[0]
    assert tc % TM == 0 and tx % TM == 0 and x.shape[2] == D_MODEL and B % SCAN_B == 0
    nc = tc // TM
    ncc = tc // CH
    S = tc + tx
    xs = jnp.concatenate([ctx, x], axis=1)
    cvec = jnp.concatenate([c, c_ctx[None, :], jnp.zeros((8 - B - 1, D_MODEL), F32)], 0)
    rope = _rope_tables(tc, tx)
    expand = _head_expand()
    vfirst = None
    for l in range(depth):
        ada = _ada_call(cvec, ada_w[l], ada_b[l][None, :]).reshape(8, 6, D_MODEL)
        mod = jnp.stack([jnp.broadcast_to(ada[B][None], (B, 6, D_MODEL)), ada[:B]], axis=1)

        w_l = _pack_w_in(w_in[l], None if l == 0 else w_vres[l - 1])
        p_rw, p_ret, p_gdn, p_lru, p_ab = _inproj_call(xs, norm_mix[l][None, :], mod, w_l, nc)

        rprm = _rwkv_params(l, rw_mu, rw_w0, rw_w2, rw_a0, rw_a2, rw_g2, rw_kk, rw_ka, rw_rk, rw_v0, rw_v2)
        sh_rw, dp_rw, bonus, g_rw = _rwkv_prep_call(p_rw, vfirst, rprm, nc)
        if l == 0:
            vfirst = sh_rw
        ys = _rwkv_scan_call(sh_rw, dp_rw, ncc)

        ros = _ret_call(p_ret, rope, _lanes_per_head(-ret_lambda[l]), ncc)

        gprm = {"conv_w": gdn_conv_w[l],
                "alog": jnp.pad(gdn_a_log[l].reshape(1, 2 * N_HEADS), ((0, 0), (0, 128 - 2 * N_HEADS))),
                "dtb": jnp.pad(gdn_dt_bias[l].reshape(1, 2 * N_HEADS), ((0, 0), (0, 128 - 2 * N_HEADS))),
                "expand": expand}
        sh_g, dp_g = _gdn_prep_call(p_gdn, p_ab, gprm, nc)
        gos = _gdn_scan_call(sh_g, dp_g, ncc)

        lprm = {"conv_w": lru_conv_w[l], "conv_b": lru_conv_b[l][None, :],
                "wa": jnp.stack([_block_diag_heads(lru_w_a[l, d]) for d in range(2)]).astype(BF16),
                "ba": lru_b_a[l],
                "wx": jnp.stack([_block_diag_heads(lru_w_x[l, d]) for d in range(2)]).astype(BF16),
                "bx": lru_b_x[l], "lam": lru_lambda[l]}
        lhs = _lru_call(p_lru, lprm, nc)

        xs = _mix_out_call(ys, bonus, g_rw, ros, p_ret, gos, p_gdn, lhs, p_lru, xs, mod, w_out[l].astype(BF16),
                           rw_ln_w[l][None, :], rw_ln_b[l][None, :], jnp.tile(gdn_norm_w[l], N_HEADS)[None, :], nc)

        rw_pad = jnp.pad(router_w[l], ((0, 0), (0, 128 - N_EXPERTS)))
        rb_pad = jnp.concatenate([router_bias[l], jnp.full((128 - N_EXPERTS,), -jnp.inf, F32)])[None, :]
        h2, idx, gate, shared = _moe_pre_call(xs, norm_ffn[l][None, :], mod, rw_pad, rb_pad,
                                              sh_w_gate[l].astype(BF16), sh_w_up[l].astype(BF16),
                                              sh_w_down[l].astype(BF16), nc)
        n_tok = B * S
        slot_tok, slot_dst, blk_e, n_rows = _route(idx[:, :, :TOP_K], n_tok)
        y = _expert_call(blk_e, slot_tok, slot_dst, h2.reshape(n_tok, D_MODEL // 2), exp_w_gate[l].astype(BF16),
                         exp_w_up[l].astype(BF16), exp_w_down[l].astype(BF16), n_rows)
        xs = _combine_call(y, gate, shared, xs, mod, nc, norm_final[None, :] if l == depth - 1 else None)
    return xs
```

```python
import functools

import jax
import jax.numpy as jnp
import numpy as np
from jax import lax
from jax.experimental import pallas as pl
from jax.experimental.pallas import tpu as pltpu

F32, BF16, I32 = jnp.float32, jnp.bfloat16, jnp.int32

D_MODEL = 1024
GROUP_W = 256
HEAD_DIM = 64
N_HEADS = GROUP_W // HEAD_DIM
HEAD_SHIFT = 6
GRID_W = 64
ROPE_BASE = 10000.0
RMS_EPS = 1e-6
GN_EPS = 64e-5
LRU_C = 8.0
N_EXPERTS = 64
TOP_K = 8
EXPERT_FF = 256
ROUTED_SCALE = 2.5
MOE_BLOCK = 128
LORA_W = 32

TM = 256
CH = 64
SCAN_B = 4
HALO = 8
COMB_T = 64
VMEM_LIMIT_BYTES = 48 * 1024 * 1024
EXPERT_VMEM_LIMIT_BYTES = 56 * 1024 * 1024

NN = (((1,), (0,)), ((), ()))
NT = (((1,), (1,)), ((), ()))
TN = (((0,), (0,)), ((), ()))
BNN = (((2,), (1,)), ((0,), (0,)))
BNT = (((2,), (2,)), ((0,), (0,)))
BTN = (((1,), (1,)), ((0,), (0,)))


def _cparams(*sem):
    return pltpu.CompilerParams(dimension_semantics=sem, vmem_limit_bytes=VMEM_LIMIT_BYTES)


def _mm(a, b, dims=NN):
    return lax.dot_general(a, b, dims, preferred_element_type=F32)


def _mmb(a, b, dims=NN):
    return _mm(a.astype(BF16), b.astype(BF16), dims)


def _split2(a):
    hi = a.astype(BF16)
    return hi, (a - hi.astype(F32)).astype(BF16)


def _split3(a):
    hi = a.astype(BF16)
    r = a - hi.astype(F32)
    mid = r.astype(BF16)
    return hi, mid, (r - mid.astype(F32)).astype(BF16)


def _mm3(a, b, dims=NN):
    ah, al = _split2(a)
    bh, bl = _split2(b)
    return _mm(ah, bh, dims) + (_mm(ah, bl, dims) + _mm(al, bh, dims))


def _mmx(a, b_exact, dims=NN):
    ah, am, al = _split3(a)
    return _mm(ah, b_exact, dims) + (_mm(am, b_exact, dims) + _mm(al, b_exact, dims))


def _xmm(a_exact, b, dims=NN):
    bh, bm, bl = _split3(b)
    return _mm(a_exact, bh, dims) + (_mm(a_exact, bm, dims) + _mm(a_exact, bl, dims))


def _iota(shape, dim):
    return lax.broadcasted_iota(I32, shape, dim)


def _head_mask(rows, cols):
    return (_iota((rows, cols), 0) >> HEAD_SHIFT) == (_iota((rows, cols), 1) >> HEAD_SHIFT)


def _head_mask_bf(rows, cols):
    return jnp.where(_head_mask(rows, cols), 1.0, 0.0).astype(BF16)


def _headsum(x, mbd_bf):
    return _mmx(x, mbd_bf)


def _bd(x, mask_bf):
    return jnp.concatenate([x.astype(BF16)] * N_HEADS, axis=1) * mask_bf


def _tri_inverse(a_all, mask_bf):
    def times(x, y):
        n = x.shape[1]
        xh, xl = _split2(x)
        yh, yl = _split2(y)
        top = _mm(jnp.concatenate([xh, xl], axis=1), _bd(yh, mask_bf), BNN)
        return (top[:, 0:n] + top[:, n:2 * n]) + _mm(xh, _bd(yl, mask_bf), BNN)

    t = _iota(a_all.shape, 1)
    j = _iota(a_all.shape, 2) & (CH - 1)
    p = jnp.where(t == j, 1.0, 0.0) + a_all
    a = times(a_all, a_all)
    levels = CH.bit_length() - 2
    for lvl in range(levels):
        if lvl + 1 < levels:
            both = times(jnp.concatenate([a, p], axis=1), a)
            a = both[:, 0:CH]
            p = p + both[:, CH:2 * CH]
        else:
            p = p + times(p, a)
    return p


def _scan_dist(ncols):
    shape = (2 * SCAN_B, CH, ncols)
    t = _iota(shape, 1)
    j = _iota(shape, 2) & (CH - 1)
    return jnp.where(_iota(shape, 0) >= SCAN_B, j - t, t - j)


def _tri_ones():
    return jnp.where(_scan_dist(CH) >= 0, 1.0, 0.0).astype(BF16)


def _chains(fwd_ref, rev_ref):
    return jnp.concatenate([fwd_ref[...], rev_ref[...]], axis=0)


def _rms_modulate(x, g, shift, scale):
    y = x * lax.rsqrt(jnp.mean(x * x, axis=-1, keepdims=True) + RMS_EPS) * g
    return y * (1.0 + scale) + shift


def _softplus(x):
    return jnp.maximum(x, 0.0) + jnp.log(1.0 + jnp.exp(-jnp.abs(x)))


def _silu(x):
    return x * jax.nn.sigmoid(x)


def _gelu_tanh(x):
    return 0.5 * x * (1.0 + jnp.tanh(np.sqrt(2.0 / np.pi).astype(np.float32) * (x + 0.044715 * (x * x * x))))


def _shift_rows(p, prev_row, next_row):
    n = p.shape[0]
    r = _iota((n, 1), 0)
    up = jnp.where(r == 0, prev_row, pltpu.roll(p, 1, 0))
    dn = jnp.where(r == n - 1, next_row, pltpu.roll(p, n - 1, 0))
    return up, dn


def _seq_edges(i, n_ctx_tiles, n_tiles):
    first = jnp.logical_or(i == 0, i == n_ctx_tiles)
    last = jnp.logical_or(i == n_ctx_tiles - 1, i == n_tiles - 1)
    return first, last


def _fill_conv_scratch(scr, cur, prev8, next8, first, last):
    n = cur.shape[0]
    scr[0:HALO, :] = jnp.where(first, 0.0, prev8)
    scr[HALO:HALO + n, :] = cur
    scr[HALO + n:2 * HALO + n, :] = jnp.where(last, 0.0, next8)


def _conv4(scr, w, n):
    return (scr[HALO - 2:HALO - 2 + n, :] * w[0:1, :] + scr[HALO - 1:HALO - 1 + n, :] * w[1:2, :]
            + scr[HALO:HALO + n, :] * w[2:3, :] + scr[HALO + 1:HALO + 1 + n, :] * w[3:4, :])


def _ada_kernel(c_ref, w_ref, b_ref, o_ref):
    c = c_ref[...]
    o_ref[...] = _mm3(_silu(c), w_ref[...]) + b_ref[...]


def _ada_call(cvec, w, b):
    n = w.shape[1]
    tn = 768
    return pl.pallas_call(
        _ada_kernel,
        grid=(n // tn,),
        in_specs=[pl.BlockSpec((8, D_MODEL), lambda j: (0, 0)),
                  pl.BlockSpec((D_MODEL, tn), lambda j: (0, j)),
                  pl.BlockSpec((1, tn), lambda j: (0, j))],
        out_specs=pl.BlockSpec((8, tn), lambda j: (0, j)),
        out_shape=jax.ShapeDtypeStruct((8, n), F32),
        compiler_params=_cparams("arbitrary"),
        name="ada_mod",
    )(cvec, w, b)


IN_COLS = (1024, 1024, 1024, 512, 128)


def _inproj_kernel(x_ref, g_ref, mod_ref, w_ref, rw_ref, ret_ref, gdn_ref, lru_ref, ab_ref):
    h = _rms_modulate(x_ref[...], g_ref[...], mod_ref[0:1, :], mod_ref[1:2, :])
    p = _mm(h.astype(BF16), w_ref[...])
    o = 0
    for ref, n in zip((rw_ref, ret_ref, gdn_ref, lru_ref, ab_ref), IN_COLS):
        ref[...] = p[:, o:o + n]
        o += n


def _mod_spec(nc):
    return pl.BlockSpec((None, None, 6, D_MODEL), lambda b, i: (b, jnp.minimum(i // nc, 1), 0, 0))


def _inproj_call(xs, gain, mod, w, nc):
    B, S, _ = xs.shape
    n = w.shape[1]
    row = lambda width: pl.BlockSpec((None, TM, width), lambda b, i: (b, i, 0))
    return pl.pallas_call(
        _inproj_kernel,
        grid=(B, S // TM),
        in_specs=[row(D_MODEL), pl.BlockSpec((1, D_MODEL), lambda b, i: (0, 0)), _mod_spec(nc),
                  pl.BlockSpec((D_MODEL, n), lambda b, i: (0, 0))],
        out_specs=[row(c) for c in IN_COLS],
        out_shape=[jax.ShapeDtypeStruct((B, S, c), F32) for c in IN_COLS],
        compiler_params=_cparams("arbitrary", "arbitrary"),
        name="in_proj",
    )(xs, gain, mod, w)


def _rwkv_prep_kernel(nc, nt, has_vres, *refs):
    if has_vres:
        (p_ref, pp_ref, pn_ref, vf_ref, mu_ref, w0_ref, w2_ref, a0_ref, a2_ref, g2_ref, kkw_ref, ka_ref, rk_ref,
         v0_ref, v2_ref, sh_ref, dp_ref, bonus_ref, gate_ref) = refs
    else:
        (p_ref, pp_ref, pn_ref, mu_ref, w0_ref, w2_ref, a0_ref, a2_ref, g2_ref, kkw_ref, ka_ref, rk_ref,
         sh_ref, dp_ref, bonus_ref, gate_ref) = refs
    i = pl.program_id(1)
    first, last = _seq_edges(i, nc, nt)
    p = p_ref[...]
    prev_row = jnp.where(first, 0.0, pp_ref[HALO - 1:HALO, :])
    next_row = jnp.where(last, 0.0, pn_ref[0:1, :])
    up, dn = _shift_rows(p, prev_row, next_row)
    ps = p + (0.5 * (up + dn) - p) * mu_ref[...]
    mbd = _head_mask(GROUP_W, GROUP_W).astype(BF16)

    r = ps[:, 0:256]
    k = ps[:, 256:512]
    v = ps[:, 512:768]
    x1 = ps[:, 768:896]
    x2 = ps[:, 896:1024]
    z = w0_ref[...] + _mmb(jnp.tanh(x1), w2_ref[...])
    lw = -np.exp(-0.5).astype(np.float32) * jax.nn.sigmoid(z)
    a = jax.nn.sigmoid(a0_ref[...] + _mmb(x1, a2_ref[...]))
    gate_ref[...] = _mmb(jax.nn.sigmoid(x2), g2_ref[...])
    kk = k * kkw_ref[...]
    kk = kk * lax.rsqrt(_headsum(kk * kk, mbd) + 1e-6)
    if has_vres:
        v = v + (vf_ref[...] - v) * jax.nn.sigmoid(v0_ref[...] + _mmb(x2, v2_ref[...]))
    sh_ref[:, 0:256] = r
    sh_ref[:, 256:512] = kk
    sh_ref[:, 512:768] = v
    ksum = jnp.zeros_like(k)
    for d in range(2):
        a_d = a[:, d * 256:(d + 1) * 256]
        kd = k * (1.0 + (a_d - 1.0) * ka_ref[...])
        ksum = ksum + kd
        dp_ref[d, :, 0:256] = lw[:, d * 256:(d + 1) * 256]
        dp_ref[d, :, 256:512] = kd
        dp_ref[d, :, 512:768] = kk * a_d
    bonus_ref[...] = _headsum(r * ksum * rk_ref[...], mbd) * v


def _halo_specs(width, nt):
    per = TM // HALO
    cur = pl.BlockSpec((None, TM, width), lambda b, i: (b, i, 0))
    prev = pl.BlockSpec((None, HALO, width), lambda b, i: (b, jnp.maximum(i * per - 1, 0), 0))
    nxt = pl.BlockSpec((None, HALO, width), lambda b, i: (b, jnp.minimum((i + 1) * per, nt * per - 1), 0))
    return cur, prev, nxt


def _const_spec(a):
    nd = a.ndim
    return pl.BlockSpec(a.shape, lambda b, i: (0,) * nd)


def _rwkv_prep_call(p_rw, vfirst_pack, prm, nc):
    B, S, _ = p_rw.shape
    nt = S // TM
    has_vres = vfirst_pack is not None
    cur, prev, nxt = _halo_specs(1024, nt)
    ins = [p_rw, p_rw, p_rw]
    specs = [cur, prev, nxt]
    if has_vres:
        ins.append(vfirst_pack)
        specs.append(pl.BlockSpec((None, TM, 256), lambda b, i: (b, i, 2)))
    names = ["mu", "w0", "w2", "a0", "a2", "g2", "kk", "ka", "rk"] + (["v0", "v2"] if has_vres else [])
    for nme in names:
        ins.append(prm[nme])
        specs.append(_const_spec(prm[nme]))
    row = lambda w: pl.BlockSpec((None, TM, w), lambda b, i: (b, i, 0))
    return pl.pallas_call(
        functools.partial(_rwkv_prep_kernel, nc, nt, has_vres),
        grid=(B, nt),
        in_specs=specs,
        out_specs=[row(768), pl.BlockSpec((None, 2, TM, 768), lambda b, i: (b, 0, i, 0)), row(256), row(256)],
        out_shape=[jax.ShapeDtypeStruct((B, S, 768), F32), jax.ShapeDtypeStruct((B, 2, S, 768), F32),
                   jax.ShapeDtypeStruct((B, S, 256), F32), jax.ShapeDtypeStruct((B, S, 256), F32)],
        compiler_params=_cparams("arbitrary", "arbitrary"),
        name="rwkv_prep",
    )(*ins)


def _rwkv_chains(sh, dp, s_ref, mask, mask_bf):
    r, kk, v = sh[:, :, 0:256], sh[:, :, 256:512], sh[:, :, 512:768]
    lw, kd, bb = dp[:, :, 0:256], dp[:, :, 256:512], dp[:, :, 512:768]
    g = _xmm(_tri_ones(), lw, BNN)
    g_tot = jnp.sum(lw, axis=1, keepdims=True)
    e_tot = jnp.exp(g_tot)
    eng = jnp.exp(-g)
    ab = -kk * jnp.exp(g - lw)
    bbar = bb * eng
    kbar = kd * eng
    rbar = r * jnp.exp(g)
    dist = _scan_dist(N_HEADS * CH)
    strict, incl = dist > 0, dist >= 0
    abb, rbb, vb = ab.astype(BF16), rbar.astype(BF16), v.astype(BF16)
    sc = _mm(jnp.concatenate([abb, rbb], axis=1),
             jnp.concatenate([_bd(bbar, mask_bf), _bd(kbar, mask_bf)], axis=1), BNT)
    a_ab = jnp.where(strict, sc[:, 0:CH, 0:256], 0.0)
    a_ak = jnp.where(strict, sc[:, 0:CH, 256:512], 0.0)
    a_rb = jnp.where(incl, sc[:, CH:2 * CH, 0:256], 0.0)
    a_rk = jnp.where(incl, sc[:, CH:2 * CH, 256:512], 0.0)
    tb = _tri_inverse(a_ab, mask_bf).astype(BF16)
    vbd = _bd(vb, mask_bf)
    av = _mm(jnp.concatenate([a_ak, a_rk], axis=1).astype(BF16), vbd, BNN)
    rhs, y0 = av[:, 0:CH], av[:, CH:2 * CH]
    wt = _mm(tb, _bd(abb, mask_bf), BNN)
    u0 = _mm(tb, _bd(rhs, mask_bf), BNN)
    s = s_ref[...]
    ws = _mm(jnp.concatenate([wt.astype(BF16), rbb], axis=1), s.astype(BF16), BNT)
    u = ws[:, 0:CH] + u0
    ub = u.astype(BF16)
    y = ws[:, CH:2 * CH] + _mm(a_rb.astype(BF16), _bd(ub, mask_bf), BNN) + y0
    upd = _mm(jnp.concatenate([ub, vb], axis=1),
              jnp.concatenate([(bbar * e_tot).astype(BF16), (kbar * e_tot).astype(BF16)], axis=1), BTN)
    s_ref[...] = s * e_tot + jnp.where(mask, upd, 0.0)
    return y


def _rwkv_scan_kernel(sh0_ref, sh1_ref, dp0_ref, dp1_ref, y0_ref, y1_ref, s_ref):
    @pl.when(pl.program_id(1) == 0)
    def _():
        s_ref[...] = jnp.zeros_like(s_ref)

    y = _rwkv_chains(_chains(sh0_ref, sh1_ref), _chains(dp0_ref, dp1_ref), s_ref,
                     _head_mask(GROUP_W, GROUP_W), _head_mask_bf(GROUP_W, GROUP_W))
    y0_ref[...] = y[0:SCAN_B]
    y1_ref[...] = y[SCAN_B:2 * SCAN_B]


def _rev_idx(n, n_ctx, n_all):
    return jnp.where(n < n_ctx, n_ctx - 1 - n, n_all + n_ctx - 1 - n)


def _scan_specs(width_sh, width_dp, ncc, nch):
    fwd = lambda b, n: (b, n, 0)
    rev = lambda b, n: (b, _rev_idx(n, ncc, nch), 0)
    fwd_d = lambda b, n: (b, 0, n, 0)
    rev_d = lambda b, n: (b, 1, _rev_idx(n, ncc, nch), 0)
    return ([pl.BlockSpec((SCAN_B, CH, width_sh), fwd), pl.BlockSpec((SCAN_B, CH, width_sh), rev),
             pl.BlockSpec((SCAN_B, None, CH, width_dp), fwd_d), pl.BlockSpec((SCAN_B, None, CH, width_dp), rev_d)],
            [pl.BlockSpec((SCAN_B, CH, GROUP_W), fwd), pl.BlockSpec((SCAN_B, CH, GROUP_W), rev)])


def _state_scratch():
    return [pltpu.VMEM((2 * SCAN_B, GROUP_W, GROUP_W), F32)]


def _rwkv_scan_call(sh, dp, ncc):
    B, S, _ = sh.shape
    nch = S // CH
    in_specs, out_specs = _scan_specs(768, 768, ncc, nch)
    return pl.pallas_call(
        _rwkv_scan_kernel,
        grid=(B // SCAN_B, nch),
        in_specs=in_specs,
        out_specs=out_specs,
        out_shape=[jax.ShapeDtypeStruct((B, S, GROUP_W), F32)] * 2,
        scratch_shapes=_state_scratch(),
        compiler_params=_cparams("arbitrary", "arbitrary"),
        name="rwkv_scan",
    )(sh, sh, dp, dp)


def _per_dir(fwd, rev):
    f = jnp.broadcast_to(fwd[None], (SCAN_B,) + fwd.shape)
    r = jnp.broadcast_to(rev[None], (SCAN_B,) + rev.shape)
    return jnp.concatenate([f, r], axis=0)


def _ret_chains(p, cs, lg, r_ref, mask, mask_bf):
    nc = 2 * SCAN_B
    p2 = p.reshape(nc * CH, 1024)
    cs2 = cs.reshape(nc * CH, 512)
    lane = _iota((nc * CH, GROUP_W), 1)
    first_half = (lane & (HEAD_DIM - 1)) < HEAD_DIM // 2
    cos, sin = cs2[:, 0:256], cs2[:, 256:512]

    def rope(t):
        swapped = jnp.where(first_half, pltpu.roll(t, GROUP_W - HEAD_DIM // 2, 1), pltpu.roll(t, HEAD_DIM // 2, 1))
        return (t * cos + swapped * sin).reshape(nc, CH, GROUP_W)

    q = rope(p2[:, 0:256])
    k = rope(p2[:, 256:512]) * (HEAD_DIM ** -0.5)
    v = p[:, :, 512:768]
    dist = _scan_dist(N_HEADS * CH)
    decay = jnp.where(dist >= 0, jnp.exp(jnp.maximum(dist, 0).astype(F32) * lg), 0.0)
    shape = (nc, CH, GROUP_W)
    tpos = _iota(shape, 1)
    done = jnp.where(_iota(shape, 0) >= SCAN_B, CH - 1 - tpos, tpos)
    xi = jnp.exp((done + 1).astype(F32) * lg)
    zeta = jnp.exp((CH - 1 - done).astype(F32) * lg)
    g_c = jnp.exp(float(CH) * lg)
    vb = v.astype(BF16)
    s = _mm(q.astype(BF16), _bd(k, mask_bf), BNT) * decay
    rr = r_ref[...]
    o = _mm(s.astype(BF16), _bd(vb, mask_bf), BNN) + _mm((q * xi).astype(BF16), rr.astype(BF16), BNN)
    r_ref[...] = rr * g_c + jnp.where(mask, _mm((k * zeta).astype(BF16), vb, BTN), 0.0)
    return o


def _ret_kernel(p0_ref, p1_ref, cs0_ref, cs1_ref, lg_ref, o0_ref, o1_ref, r_ref):
    @pl.when(pl.program_id(1) == 0)
    def _():
        r_ref[...] = jnp.zeros_like(r_ref)

    o = _ret_chains(_chains(p0_ref, p1_ref), _per_dir(cs0_ref[...], cs1_ref[...]),
                    _per_dir(lg_ref[0:1, :], lg_ref[1:2, :]), r_ref,
                    _head_mask(GROUP_W, GROUP_W), _head_mask_bf(GROUP_W, GROUP_W))
    o0_ref[...] = o[0:SCAN_B]
    o1_ref[...] = o[SCAN_B:2 * SCAN_B]


def _ret_call(p_ret, cs, lg, ncc):
    B, S, _ = p_ret.shape
    nch = S // CH
    fwd = lambda b, n: (b, n, 0)
    rev = lambda b, n: (b, _rev_idx(n, ncc, nch), 0)
    fwd2 = lambda b, n: (n, 0)
    rev2 = lambda b, n: (_rev_idx(n, ncc, nch), 0)
    return pl.pallas_call(
        _ret_kernel,
        grid=(B // SCAN_B, nch),
        in_specs=[pl.BlockSpec((SCAN_B, CH, 1024), fwd), pl.BlockSpec((SCAN_B, CH, 1024), rev),
                  pl.BlockSpec((CH, 512), fwd2), pl.BlockSpec((CH, 512), rev2),
                  pl.BlockSpec((2, GROUP_W), lambda b, n: (0, 0))],
        out_specs=[pl.BlockSpec((SCAN_B, CH, GROUP_W), fwd), pl.BlockSpec((SCAN_B, CH, GROUP_W), rev)],
        out_shape=[jax.ShapeDtypeStruct((B, S, GROUP_W), F32)] * 2,
        scratch_shapes=_state_scratch(),
        compiler_params=_cparams("arbitrary", "arbitrary"),
        name="retention",
    )(p_ret, p_ret, cs, cs, lg)


def _gdn_prep_kernel(nc, nt, p_ref, pp_ref, pn_ref, ab_ref, cw_ref, alog_ref, dtb_ref, e_ref, sh_ref, dp_ref, scr):
    i = pl.program_id(1)
    first, last = _seq_edges(i, nc, nt)
    _fill_conv_scratch(scr, p_ref[:, 0:768], pp_ref[:, 0:768], pn_ref[:, 0:768], first, last)
    qkv = _silu(_conv4(scr, cw_ref[...], TM))
    mbd = _head_mask(GROUP_W, GROUP_W).astype(BF16)
    q, k = qkv[:, 0:256], qkv[:, 256:512]
    sh_ref[:, 0:256] = q * lax.rsqrt(_headsum(q * q, mbd) + 1e-6) * (HEAD_DIM ** -0.5)
    sh_ref[:, 256:512] = k * lax.rsqrt(_headsum(k * k, mbd) + 1e-6)
    sh_ref[:, 512:768] = qkv[:, 512:768]
    ab = ab_ref[...]
    ld = -jnp.exp(alog_ref[...]) * _softplus(ab + dtb_ref[...])
    vec = jnp.where(_iota(ab.shape, 1) < 2 * N_HEADS, ld, jax.nn.sigmoid(ab))
    ex = _mmx(vec, e_ref[...])
    for d in range(2):
        dp_ref[d, :, 0:256] = ex[:, d * 256:(d + 1) * 256]
        dp_ref[d, :, 256:512] = ex[:, 512 + d * 256:512 + (d + 1) * 256]


def _gdn_prep_call(p_gdn, p_ab, prm, nc):
    B, S, _ = p_gdn.shape
    nt = S // TM
    cur, prev, nxt = _halo_specs(1024, nt)
    consts = [prm["conv_w"], prm["alog"], prm["dtb"], prm["expand"]]
    row = lambda w: pl.BlockSpec((None, TM, w), lambda b, i: (b, i, 0))
    return pl.pallas_call(
        functools.partial(_gdn_prep_kernel, nc, nt),
        grid=(B, nt),
        in_specs=[cur, prev, nxt, row(128)] + [_const_spec(a) for a in consts],
        out_specs=[row(768), pl.BlockSpec((None, 2, TM, 512), lambda b, i: (b, 0, i, 0))],
        out_shape=[jax.ShapeDtypeStruct((B, S, 768), F32), jax.ShapeDtypeStruct((B, 2, S, 512), F32)],
        scratch_shapes=[pltpu.VMEM((TM + 2 * HALO, 768), F32)],
        compiler_params=_cparams("arbitrary", "arbitrary"),
        name="gdn_prep",
    )(p_gdn, p_gdn, p_gdn, p_ab, *consts)


def _gdn_chains(sh, dp, s_ref, mask, mask_bf):
    q, k, v = sh[:, :, 0:256], sh[:, :, 256:512], sh[:, :, 512:768]
    ld, beta = dp[:, :, 0:256], dp[:, :, 256:512]
    gc = _xmm(_tri_ones(), ld, BNN)
    g_tot = jnp.sum(ld, axis=1, keepdims=True)
    dist = _scan_dist(N_HEADS * CH)
    gc_cols = jnp.sum(jnp.where(dist == 0, gc, 0.0), axis=1, keepdims=True)
    gam = jnp.where(dist >= 0, jnp.exp(jnp.minimum(gc - gc_cols, 0.0)), 0.0)
    kb = k * beta
    kk = _mm(jnp.concatenate([kb, q], axis=1).astype(BF16), _bd(k, mask_bf), BNT)
    a_mat = jnp.where(dist > 0, kk[:, 0:CH] * gam, 0.0)
    attn = kk[:, CH:2 * CH] * gam
    tb = _tri_inverse(-a_mat, mask_bf).astype(BF16)
    egc = jnp.exp(gc)
    u = _mm(tb, _bd(v * beta, mask_bf), BNN)
    w = _mm(tb, _bd(kb * egc, mask_bf), BNN)
    s = s_ref[...]
    ws = _mm(jnp.concatenate([w, q * egc], axis=1).astype(BF16), s.astype(BF16), BNN)
    v_new = u - ws[:, 0:CH]
    vnb = v_new.astype(BF16)
    o = ws[:, CH:2 * CH] + _mm(attn.astype(BF16), _bd(vnb, mask_bf), BNN)
    upd = _mm((k * jnp.exp(g_tot - gc)).astype(BF16), vnb, BTN)
    s_ref[...] = s * jnp.exp(g_tot) + jnp.where(mask, upd, 0.0)
    return o


def _gdn_scan_kernel(sh0_ref, sh1_ref, dp0_ref, dp1_ref, o0_ref, o1_ref, s_ref):
    @pl.when(pl.program_id(1) == 0)
    def _():
        s_ref[...] = jnp.zeros_like(s_ref)

    o = _gdn_chains(_chains(sh0_ref, sh1_ref), _chains(dp0_ref, dp1_ref), s_ref,
                    _head_mask(GROUP_W, GROUP_W), _head_mask_bf(GROUP_W, GROUP_W))
    o0_ref[...] = o[0:SCAN_B]
    o1_ref[...] = o[SCAN_B:2 * SCAN_B]


def _gdn_scan_call(sh, dp, ncc):
    B, S, _ = sh.shape
    nch = S // CH
    in_specs, out_specs = _scan_specs(768, 512, ncc, nch)
    return pl.pallas_call(
        _gdn_scan_kernel,
        grid=(B // SCAN_B, nch),
        in_specs=in_specs,
        out_specs=out_specs,
        out_shape=[jax.ShapeDtypeStruct((B, S, GROUP_W), F32)] * 2,
        scratch_shapes=_state_scratch(),
        compiler_params=_cparams("arbitrary", "arbitrary"),
        name="gdn_scan",
    )(sh, sh, dp, dp)


def _lru_tile(d, rev, first, last, p_ref, pp_ref, pn_ref, cw_ref, cb_ref, wa_ref, ba_ref, wx_ref, bx_ref, lam_ref,
              h_ref, carry_ref, scr):
    _fill_conv_scratch(scr, p_ref[:, 0:256], pp_ref[:, 0:256], pn_ref[:, 0:256], first, last)
    xm = _conv4(scr, cw_ref[...], TM) + cb_ref[...]
    xb = xm.astype(BF16)
    r = jax.nn.sigmoid(_mm(xb, wa_ref[d]) + ba_ref[d:d + 1, :])
    ig = jax.nn.sigmoid(_mm(xb, wx_ref[d]) + bx_ref[d:d + 1, :])
    log_a = -LRU_C * _softplus(-lam_ref[d:d + 1, :]) * r
    a = jnp.exp(log_a)
    th = jnp.tanh(log_a)
    b = jnp.sqrt(-2.0 * th / (1.0 - th)) * ig * xm
    row = _iota((TM, 1), 0)
    s = 1
    while s < TM:
        if rev:
            ok = row < TM - s
            a_n, b_n = pltpu.roll(a, TM - s, 0), pltpu.roll(b, TM - s, 0)
        else:
            ok = row >= s
            a_n, b_n = pltpu.roll(a, s, 0), pltpu.roll(b, s, 0)
        b = jnp.where(ok, b + a * b_n, b)
        a = jnp.where(ok, a * a_n, a)
        s *= 2
    h = b + a * carry_ref[0:1, :]
    h_ref[...] = h
    end = h[0:1, :] if rev else h[TM - 1:TM, :]
    carry_ref[...] = jnp.broadcast_to(end, carry_ref.shape)


def _lru_kernel(nc, nt, p0, pp0, pn0, p1, pp1, pn1, cw, cb, wa, ba, wx, bx, lam, h0_ref, h1_ref, c0, c1, scr):
    i = pl.program_id(1)

    @pl.when(i == 0)
    def _():
        c0[...] = jnp.zeros_like(c0)
        c1[...] = jnp.zeros_like(c1)

    first, last = _seq_edges(i, nc, nt)
    _lru_tile(0, False, first, last, p0, pp0, pn0, cw, cb, wa, ba, wx, bx, lam, h0_ref, c0, scr)
    j = _rev_idx(i, nc, nt)
    first, last = _seq_edges(j, nc, nt)
    _lru_tile(1, True, first, last, p1, pp1, pn1, cw, cb, wa, ba, wx, bx, lam, h1_ref, c1, scr)


def _lru_call(p_lru, prm, nc):
    B, S, _ = p_lru.shape
    nt = S // TM
    per = TM // HALO
    rj = lambda i: _rev_idx(i, nc, nt)
    cur, prev, nxt = _halo_specs(512, nt)
    cur_r = pl.BlockSpec((None, TM, 512), lambda b, i: (b, rj(i), 0))
    prev_r = pl.BlockSpec((None, HALO, 512), lambda b, i: (b, jnp.maximum(rj(i) * per - 1, 0), 0))
    nxt_r = pl.BlockSpec((None, HALO, 512), lambda b, i: (b, jnp.minimum((rj(i) + 1) * per, nt * per - 1), 0))
    consts = [prm[k] for k in ("conv_w", "conv_b", "wa", "ba", "wx", "bx", "lam")]
    return pl.pallas_call(
        functools.partial(_lru_kernel, nc, nt),
        grid=(B, nt),
        in_specs=[cur, prev, nxt, cur_r, prev_r, nxt_r] + [_const_spec(a) for a in consts],
        out_specs=[pl.BlockSpec((None, TM, GROUP_W), lambda b, i: (b, i, 0)),
                   pl.BlockSpec((None, TM, GROUP_W), lambda b, i: (b, rj(i), 0))],
        out_shape=[jax.ShapeDtypeStruct((B, S, GROUP_W), F32)] * 2,
        scratch_shapes=[pltpu.VMEM((8, GROUP_W), F32), pltpu.VMEM((8, GROUP_W), F32),
                        pltpu.VMEM((TM + 2 * HALO, GROUP_W), F32)],
        compiler_params=_cparams("arbitrary", "arbitrary"),
        name="rglru",
    )(p_lru, p_lru, p_lru, p_lru, p_lru, p_lru, *consts)


def _mix_out_kernel(y0, y1, bonus, grw, ro0, ro1, gret, go0, go1, ggdn, lh0, lh1, glru, x_ref, mod_ref, w_ref,
                    lnw, lnb, gnw, o_ref):
    mbd = _head_mask(GROUP_W, GROUP_W).astype(BF16)
    inv = 1.0 / HEAD_DIM

    def head_rms(o):
        return o * lax.rsqrt(_headsum(o * o, mbd) * inv + RMS_EPS)

    y = y0[...] + y1[...]
    mu = _headsum(y, mbd) * inv
    dy = y - mu
    var = _headsum(dy * dy, mbd) * inv
    a = (dy * lax.rsqrt(var + GN_EPS) * lnw[...] + lnb[...] + bonus[...]) * grw[...]
    b = head_rms(ro0[...] + ro1[...]) * _silu(gret[...])
    g = head_rms(go0[...] + go1[...]) * gnw[...] * _silu(ggdn[...])
    r = (lh0[...] + lh1[...]) * _gelu_tanh(glru[...])
    w = w_ref[...]
    mix = (_mm(a.astype(BF16), w[0:256]) + _mm(b.astype(BF16), w[256:512])
           + _mm(g.astype(BF16), w[512:768]) + _mm(r.astype(BF16), w[768:1024]))
    o_ref[...] = x_ref[...] + mod_ref[2:3, :] * mix


def _mix_out_call(ys, bonus, grw, ros, p_ret, gos, p_gdn, lhs, p_lru, xs, mod, w_out, lnw, lnb, gnw, nc):
    B, S, _ = xs.shape
    row = lambda w, c=0: pl.BlockSpec((None, TM, w), lambda b, i: (b, i, c))
    g = row(GROUP_W)
    consts = [lnw, lnb, gnw]
    return pl.pallas_call(
        _mix_out_kernel,
        grid=(B, S // TM),
        in_specs=[g, g, g, g, g, g, row(GROUP_W, 3), g, g, row(GROUP_W, 3), g, g, row(GROUP_W, 1),
                  row(D_MODEL), _mod_spec(nc), pl.BlockSpec((D_MODEL, D_MODEL), lambda b, i: (0, 0))]
                 + [_const_spec(a) for a in consts],
        out_specs=row(D_MODEL),
        out_shape=jax.ShapeDtypeStruct((B, S, D_MODEL), F32),
        compiler_params=_cparams("arbitrary", "arbitrary"),
        name="mix_out",
    )(ys[0], ys[1], bonus, grw, ros[0], ros[1], p_ret, gos[0], gos[1], p_gdn, lhs[0], lhs[1], p_lru,
      xs, mod, w_out, *consts)


def _moe_pre_kernel(x_ref, g_ref, mod_ref, rw_ref, rb_ref, sg_ref, su_ref, sd_ref, h_ref, idx_ref, gate_ref, sh_ref):
    h = _rms_modulate(x_ref[...], g_ref[...], mod_ref[3:4, :], mod_ref[4:5, :])
    hb = h.astype(BF16)
    bits = pltpu.bitcast(hb.astype(F32), jnp.uint32)
    half = D_MODEL // 2
    h_ref[...] = (bits[:, 0:half] >> 16) | (bits[:, half:D_MODEL] & jnp.uint32(0xFFFF0000))
    scores = jax.nn.sigmoid(_mm3(h, rw_ref[...]))
    sel = scores + rb_ref[...]
    lane = _iota(scores.shape, 1)
    gates = jnp.zeros_like(scores)
    idxs = jnp.zeros(scores.shape, I32)
    total = jnp.zeros((scores.shape[0], 1), F32)
    for kk in range(TOP_K):
        m = jnp.max(sel, axis=-1, keepdims=True)
        idx = jnp.min(jnp.where(sel == m, lane, 2 * N_EXPERTS), axis=-1, keepdims=True)
        hit = lane == idx
        gk = jnp.sum(jnp.where(hit, scores, 0.0), axis=-1, keepdims=True)
        total = total + gk
        gates = jnp.where(lane == kk, gk, gates)
        idxs = jnp.where(lane == kk, idx, idxs)
        sel = jnp.where(hit, -jnp.inf, sel)
    gate_ref[...] = ROUTED_SCALE * gates / total
    idx_ref[...] = idxs
    act = _silu(_mm(hb, sg_ref[...])) * _mm(hb, su_ref[...])
    sh_ref[...] = _mm(act.astype(BF16), sd_ref[...])


def _moe_pre_call(xs, gain, mod, rw, rb, sg, su, sd, nc):
    B, S, _ = xs.shape
    row = lambda w: pl.BlockSpec((None, TM, w), lambda b, i: (b, i, 0))
    consts = [rw, rb, sg, su, sd]
    return pl.pallas_call(
        _moe_pre_kernel,
        grid=(B, S // TM),
        in_specs=[row(D_MODEL), pl.BlockSpec((1, D_MODEL), lambda b, i: (0, 0)), _mod_spec(nc)]
                 + [_const_spec(a) for a in consts],
        out_specs=[row(D_MODEL // 2), row(128), row(128), row(D_MODEL)],
        out_shape=[jax.ShapeDtypeStruct((B, S, D_MODEL // 2), jnp.uint32), jax.ShapeDtypeStruct((B, S, 128), I32),
                   jax.ShapeDtypeStruct((B, S, 128), F32), jax.ShapeDtypeStruct((B, S, D_MODEL), F32)],
        compiler_params=_cparams("arbitrary", "arbitrary"),
        name="moe_pre",
    )(xs, gain, mod, *consts)


def _row_out_copy(src, dst_hbm, sem, r, dst_row):
    return pltpu.make_async_copy(src.at[pl.ds(r, 1), :], dst_hbm.at[pl.ds(dst_row, 1), :], sem)


def _wait_block_out(yb_slot, y_hbm, sem):
    for r in range(MOE_BLOCK):
        _row_out_copy(yb_slot, y_hbm, sem, r, 0).wait()


def _start_block_out(yb_slot, y_hbm, sem, dst_ref):
    for r in range(MOE_BLOCK):
        _row_out_copy(yb_slot, y_hbm, sem, r, dst_ref[0, 0, r]).start()


def _expert_kernel(blk_e_ref, tok_ref, dst_ref, dstp_ref, hp_hbm, wg_ref, wu_ref, wd_ref, y_hbm,
                   tab, xb, yb, tsem, ssem):
    i = pl.program_id(0)
    nb = pl.num_programs(0)
    slot = lax.rem(i, 2)
    other = 1 - slot

    @pl.when(i == 0)
    def _():
        load = pltpu.make_async_copy(hp_hbm, tab, tsem)
        load.start()
        yb[1] = jnp.zeros((MOE_BLOCK, D_MODEL), F32)
        load.wait()

    _start_block_out(yb.at[other], y_hbm, ssem.at[other], dstp_ref)
    for r in range(MOE_BLOCK):
        xb[pl.ds(r, 1), :] = tab[pl.ds(tok_ref[0, 0, r], 1), :]
    u = xb[...]
    lo = pltpu.bitcast(u << 16, F32).astype(BF16)
    hi = pltpu.bitcast(u & jnp.uint32(0xFFFF0000), F32).astype(BF16)
    x = jnp.concatenate([lo, hi], axis=1)
    act = _silu(_mm(x, wg_ref[...])) * _mm(x, wu_ref[...])
    y = _mm(act.astype(BF16), wd_ref[...])

    @pl.when(i >= 1)
    def _():
        _wait_block_out(yb.at[slot], y_hbm, ssem.at[slot])

    yb[slot] = y

    @pl.when(i == nb - 1)
    def _():
        _start_block_out(yb.at[slot], y_hbm, ssem.at[slot], dst_ref)
        _wait_block_out(yb.at[other], y_hbm, ssem.at[other])
        _wait_block_out(yb.at[slot], y_hbm, ssem.at[slot])


def _expert_call(blk_e, slot_tok, slot_dst, dst_prev, hp_flat, wg, wu, wd, n_rows):
    nb = blk_e.shape[0]
    ff = wg.shape[2]
    n_tok, half = hp_flat.shape
    idx_spec = pl.BlockSpec((1, 1, MOE_BLOCK), lambda i, be: (i, 0, 0), memory_space=pltpu.SMEM)
    grid_spec = pltpu.PrefetchScalarGridSpec(
        num_scalar_prefetch=1,
        grid=(nb,),
        in_specs=[idx_spec, idx_spec, idx_spec,
                  pl.BlockSpec(memory_space=pl.ANY),
                  pl.BlockSpec((None, D_MODEL, ff), lambda i, be: (be[i], 0, 0)),
                  pl.BlockSpec((None, D_MODEL, ff), lambda i, be: (be[i], 0, 0)),
                  pl.BlockSpec((None, ff, D_MODEL), lambda i, be: (be[i], 0, 0))],
        out_specs=pl.BlockSpec(memory_space=pl.ANY),
        scratch_shapes=[pltpu.VMEM((n_tok, half), jnp.uint32), pltpu.VMEM((MOE_BLOCK, half), jnp.uint32),
                        pltpu.VMEM((2, MOE_BLOCK, D_MODEL), F32),
                        pltpu.SemaphoreType.DMA(()), pltpu.SemaphoreType.DMA((2,))],
    )
    return pl.pallas_call(
        _expert_kernel,
        grid_spec=grid_spec,
        out_shape=jax.ShapeDtypeStruct((n_rows, D_MODEL), F32),
        compiler_params=pltpu.CompilerParams(dimension_semantics=("arbitrary",),
                                             vmem_limit_bytes=EXPERT_VMEM_LIMIT_BYTES),
        name="moe_experts",
    )(blk_e, slot_tok, slot_dst, dst_prev, hp_flat, wg, wu, wd)


def _combined(y_ref, gate_ref, sh_ref, x_ref, mod_ref):
    gate = gate_ref[...]
    routed = jnp.zeros((COMB_T, D_MODEL), F32)
    for k in range(TOP_K):
        routed = routed + gate[:, k:k + 1] * y_ref[k * COMB_T:(k + 1) * COMB_T, :]
    return x_ref[...] + mod_ref[5:6, :] * (routed + sh_ref[...])


def _combine_kernel(y_ref, gate_ref, sh_ref, x_ref, mod_ref, o_ref):
    o_ref[...] = _combined(y_ref, gate_ref, sh_ref, x_ref, mod_ref)


def _combine_norm_kernel(y_ref, gate_ref, sh_ref, x_ref, mod_ref, g_ref, o_ref):
    x = _combined(y_ref, gate_ref, sh_ref, x_ref, mod_ref)
    o_ref[...] = x * lax.rsqrt(jnp.mean(x * x, axis=-1, keepdims=True) + RMS_EPS) * g_ref[...]


def _combine_call(y, gate, shared, xs, mod, nc, final_gain=None):
    B, S, _ = xs.shape
    per = S // COMB_T
    skip = 0 if final_gain is None else nc * TM // COMB_T
    row = lambda w: pl.BlockSpec((None, COMB_T, w), lambda b, i: (b, i + skip, 0))
    mod_spec = pl.BlockSpec((None, None, 6, D_MODEL),
                            lambda b, i: (b, jnp.minimum((i + skip) // (nc * TM // COMB_T), 1), 0, 0))
    in_specs = [pl.BlockSpec((COMB_T * TOP_K, D_MODEL), lambda b, i: (b * per + i + skip, 0)),
                row(128), row(D_MODEL), row(D_MODEL), mod_spec]
    args = [y, gate, shared, xs, mod]
    if final_gain is not None:
        in_specs.append(pl.BlockSpec((1, D_MODEL), lambda b, i: (0, 0)))
        args.append(final_gain)
    return pl.pallas_call(
        _combine_kernel if final_gain is None else _combine_norm_kernel,
        grid=(B, per - skip),
        in_specs=in_specs,
        out_specs=pl.BlockSpec((None, COMB_T, D_MODEL), lambda b, i: (b, i, 0)),
        out_shape=jax.ShapeDtypeStruct((B, S - skip * COMB_T, D_MODEL), F32),
        compiler_params=_cparams("arbitrary", "arbitrary"),
        name="moe_combine",
    )(*args)


N_RWKV, N_RET, N_GDN, N_LRU = 960, 1024, 1040, 512


def _pack_w_in(w_in_l, w_vres_l):
    o_ret = N_RWKV
    o_gdn = o_ret + N_RET
    o_lru = o_gdn + N_GDN
    z = lambda n: jnp.zeros((D_MODEL, n), F32)
    vres = z(LORA_W) if w_vres_l is None else w_vres_l
    cols = [w_in_l[:, 0:N_RWKV], vres, z(1024 - N_RWKV - LORA_W),
            w_in_l[:, o_ret:o_gdn],
            w_in_l[:, o_gdn:o_gdn + 768], w_in_l[:, o_gdn + 784:o_lru],
            w_in_l[:, o_lru:o_lru + N_LRU],
            w_in_l[:, o_gdn + 768:o_gdn + 784], z(112)]
    return jnp.concatenate(cols, axis=1).astype(BF16)


def _block_diag2(w):
    r, c = w.shape[1], w.shape[2]
    z = jnp.zeros((r, c), F32)
    return jnp.concatenate([jnp.concatenate([w[0], z], 1), jnp.concatenate([z, w[1]], 1)], 0)


def _rwkv_params(l, rw_mu, rw_w0, rw_w2, rw_a0, rw_a2, rw_g2, rw_kk, rw_ka, rw_rk, rw_v0, rw_v2):
    z64 = jnp.zeros((2 * LORA_W, 2 * GROUP_W), F32)
    prm = {
        "mu": jnp.pad(rw_mu[l], (0, 1024 - N_RWKV))[None, :],
        "w0": rw_w0[l].reshape(1, 2 * GROUP_W),
        "w2": jnp.concatenate([_block_diag2(rw_w2[l]), z64], 0),
        "a0": rw_a0[l].reshape(1, 2 * GROUP_W),
        "a2": jnp.concatenate([z64, _block_diag2(rw_a2[l])], 0),
        "g2": jnp.concatenate([rw_g2[l], jnp.zeros((64, GROUP_W), F32)], 0),
        "kk": rw_kk[l][None, :],
        "ka": rw_ka[l][None, :],
        "rk": rw_rk[l].reshape(1, GROUP_W),
    }
    if l > 0:
        prm["v0"] = rw_v0[l - 1][None, :]
        prm["v2"] = jnp.concatenate([jnp.zeros((64, GROUP_W), F32), rw_v2[l - 1],
                                     jnp.zeros((128 - 64 - LORA_W, GROUP_W), F32)], 0)
    return prm


def _head_expand():
    e = np.zeros((128, 1024), np.float32)
    for c in range(4 * N_HEADS):
        grp, h = divmod(c, N_HEADS)
        e[c, grp * GROUP_W + h * HEAD_DIM: grp * GROUP_W + (h + 1) * HEAD_DIM] = 1.0
    return jnp.asarray(e, BF16)


def _lanes_per_head(v):
    return jnp.repeat(v, HEAD_DIM, axis=-1)


def _block_diag_heads(w):
    out = jnp.zeros((GROUP_W, GROUP_W), F32)
    for h in range(N_HEADS):
        out = out.at[h * HEAD_DIM:(h + 1) * HEAD_DIM, h * HEAD_DIM:(h + 1) * HEAD_DIM].set(w[h])
    return out


def _rope_tables(tc, tx):
    t = np.arange(tx)
    rows = (t // GRID_W).astype(np.float32)
    cols = (t % GRID_W).astype(np.float32)
    nf = HEAD_DIM // 4
    inv = (ROPE_BASE ** (-np.arange(nf, dtype=np.float32) / nf)).astype(np.float32)
    ang = jnp.concatenate([jnp.asarray(rows)[:, None] * inv, jnp.asarray(cols)[:, None] * inv], -1)
    cos, sin = jnp.cos(ang), jnp.sin(ang)
    cos_h = jnp.concatenate([cos, cos], -1)
    sin_h = jnp.concatenate([-sin, sin], -1)
    cos_f = jnp.tile(cos_h, (1, N_HEADS))
    sin_f = jnp.tile(sin_h, (1, N_HEADS))
    cos_f = jnp.concatenate([jnp.ones((tc, GROUP_W), F32), cos_f], 0)
    sin_f = jnp.concatenate([jnp.zeros((tc, GROUP_W), F32), sin_f], 0)
    return jnp.concatenate([cos_f, sin_f], -1)


def _route(idx8, n_tok):
    tk = n_tok * TOP_K
    flat_e = idx8.reshape(tk)
    order = jnp.argsort(flat_e).astype(I32)
    experts = jnp.arange(N_EXPERTS, dtype=I32)
    counts = jnp.sum((flat_e[:, None] == experts[None, :]).astype(I32), axis=0)
    nblk = (counts + MOE_BLOCK - 1) // MOE_BLOCK
    blk_end = jnp.cumsum(nblk)
    blk_start = blk_end - nblk
    grp_start = jnp.cumsum(counts) - counts
    nb = tk // MOE_BLOCK + N_EXPERTS
    bi = jnp.arange(nb, dtype=I32)
    blk_e = jnp.minimum(jnp.sum((blk_end[None, :] <= bi[:, None]).astype(I32), axis=1), N_EXPERTS - 1)
    j = bi - blk_start[blk_e]
    n_valid = jnp.clip(counts[blk_e] - j * MOE_BLOCK, 0, MOE_BLOCK)
    r = jnp.arange(MOE_BLOCK, dtype=I32)
    sorted_pos = (grp_start[blk_e] + j * MOE_BLOCK)[:, None] + r[None, :]
    valid = r[None, :] < n_valid[:, None]
    flat = order[jnp.clip(sorted_pos, 0, tk - 1)]
    tok = flat // TOP_K
    choice = flat - tok * TOP_K
    dst = (tok // COMB_T) * (COMB_T * TOP_K) + choice * COMB_T + tok % COMB_T
    spare = tk + (bi % 2)[:, None] * MOE_BLOCK + r[None, :]
    slot_dst = jnp.where(valid, dst, spare).astype(I32)
    slot_tok = jnp.where(valid, tok, 0).astype(I32)
    first = (tk + MOE_BLOCK + r)[None, :].astype(I32)
    dst_prev = jnp.concatenate([first, slot_dst[:-1]], axis=0)
    return (slot_tok.reshape(nb, 1, MOE_BLOCK), slot_dst.reshape(nb, 1, MOE_BLOCK),
            dst_prev.reshape(nb, 1, MOE_BLOCK), blk_e.astype(I32), tk + 2 * MOE_BLOCK)


def kernel(x, c, ctx, c_ctx, ada_w, ada_b, norm_mix, norm_ffn, norm_final, w_in, w_vres, w_out, rw_mu, rw_w0, rw_w2, rw_a0, rw_a2, rw_g2, rw_kk, rw_ka, rw_rk, rw_ln_w, rw_ln_b, rw_v0, rw_v2, ret_lambda, gdn_conv_w, gdn_a_log, gdn_dt_bias, gdn_norm_w, lru_conv_w, lru_conv_b, lru_w_a, lru_b_a, lru_w_x, lru_b_x, lru_lambda, router_w, router_bias, exp_w_gate, exp_w_up, exp_w_down, sh_w_gate, sh_w_up, sh_w_down):
    B, tx, _ = x.shape
    tc = ctx.shape[1]
    depth = w_in.shape[0]
    assert tc % TM == 0 and tx % TM == 0 and x.shape[2] == D_MODEL and B % SCAN_B == 0
    nc = tc // TM
    ncc = tc // CH
    S = tc + tx
    xs = jnp.concatenate([ctx, x], axis=1)
    cvec = jnp.concatenate([c, c_ctx[None, :], jnp.zeros((8 - B - 1, D_MODEL), F32)], 0)
    rope = _rope_tables(tc, tx)
    expand = _head_expand()
    vfirst = None
    for l in range(depth):
        ada = _ada_call(cvec, ada_w[l], ada_b[l][None, :]).reshape(8, 6, D_MODEL)
        mod = jnp.stack([jnp.broadcast_to(ada[B][None], (B, 6, D_MODEL)), ada[:B]], axis=1)

        w_l = _pack_w_in(w_in[l], None if l == 0 else w_vres[l - 1])
        p_rw, p_ret, p_gdn, p_lru, p_ab = _inproj_call(xs, norm_mix[l][None, :], mod, w_l, nc)

        rprm = _rwkv_params(l, rw_mu, rw_w0, rw_w2, rw_a0, rw_a2, rw_g2, rw_kk, rw_ka, rw_rk, rw_v0, rw_v2)
        sh_rw, dp_rw, bonus, g_rw = _rwkv_prep_call(p_rw, vfirst, rprm, nc)
        if l == 0:
            vfirst = sh_rw
        ys = _rwkv_scan_call(sh_rw, dp_rw, ncc)

        ros = _ret_call(p_ret, rope, _lanes_per_head(-ret_lambda[l]), ncc)

        gprm = {"conv_w": gdn_conv_w[l],
                "alog": jnp.pad(gdn_a_log[l].reshape(1, 2 * N_HEADS), ((0, 0), (0, 128 - 2 * N_HEADS))),
                "dtb": jnp.pad(gdn_dt_bias[l].reshape(1, 2 * N_HEADS), ((0, 0), (0, 128 - 2 * N_HEADS))),
                "expand": expand}
        sh_g, dp_g = _gdn_prep_call(p_gdn, p_ab, gprm, nc)
        gos = _gdn_scan_call(sh_g, dp_g, ncc)

        lprm = {"conv_w": lru_conv_w[l], "conv_b": lru_conv_b[l][None, :],
                "wa": jnp.stack([_block_diag_heads(lru_w_a[l, d]) for d in range(2)]).astype(BF16),
                "ba": lru_b_a[l],
                "wx": jnp.stack([_block_diag_heads(lru_w_x[l, d]) for d in range(2)]).astype(BF16),
                "bx": lru_b_x[l], "lam": lru_lambda[l]}
        lhs = _lru_call(p_lru, lprm, nc)

        xs = _mix_out_call(ys, bonus, g_rw, ros, p_ret, gos, p_gdn, lhs, p_lru, xs, mod, w_out[l].astype(BF16),
                           rw_ln_w[l][None, :], rw_ln_b[l][None, :], jnp.tile(gdn_norm_w[l], N_HEADS)[None, :], nc)

        rw_pad = jnp.pad(router_w[l], ((0, 0), (0, 128 - N_EXPERTS)))
        rb_pad = jnp.concatenate([router_bias[l], jnp.full((128 - N_EXPERTS,), -jnp.inf, F32)])[None, :]
        h2, idx, gate, shared = _moe_pre_call(xs, norm_ffn[l][None, :], mod, rw_pad, rb_pad,
                                              sh_w_gate[l].astype(BF16), sh_w_up[l].astype(BF16),
                                              sh_w_down[l].astype(BF16), nc)
        n_tok = B * S
        slot_tok, slot_dst, dst_prev, blk_e, n_rows = _route(idx[:, :, :TOP_K], n_tok)
        y = _expert_call(blk_e, slot_tok, slot_dst, dst_prev, h2.reshape(n_tok, D_MODEL // 2), exp_w_gate[l].astype(BF16),
                         exp_w_up[l].astype(BF16), exp_w_down[l].astype(BF16), n_rows)
        xs = _combine_call(y, gate, shared, xs, mod, nc, norm_final[None, :] if l == depth - 1 else None)
    return xs
```

```python
import functools

import jax
import jax.numpy as jnp
import numpy as np
from jax import lax
from jax.experimental import pallas as pl
from jax.experimental.pallas import tpu as pltpu

F32, BF16, I32 = jnp.float32, jnp.bfloat16, jnp.int32

D_MODEL = 1024
GROUP_W = 256
HEAD_DIM = 64
N_HEADS = GROUP_W // HEAD_DIM
HEAD_SHIFT = 6
GRID_W = 64
ROPE_BASE = 10000.0
RMS_EPS = 1e-6
GN_EPS = 64e-5
LRU_C = 8.0
N_EXPERTS = 64
TOP_K = 8
EXPERT_FF = 256
ROUTED_SCALE = 2.5
MOE_BLOCK = 256
LORA_W = 32

TM = 256
CH = 64
SCAN_B = 4
HALO = 8
COMB_T = 128
VMEM_LIMIT_BYTES = 48 * 1024 * 1024
EXPERT_VMEM_LIMIT_BYTES = 56 * 1024 * 1024

NN = (((1,), (0,)), ((), ()))
NT = (((1,), (1,)), ((), ()))
TN = (((0,), (0,)), ((), ()))
BNN = (((2,), (1,)), ((0,), (0,)))
BNT = (((2,), (2,)), ((0,), (0,)))
BTN = (((1,), (1,)), ((0,), (0,)))


def _cparams(*sem):
    return pltpu.CompilerParams(dimension_semantics=sem, vmem_limit_bytes=VMEM_LIMIT_BYTES)


def _mm(a, b, dims=NN):
    return lax.dot_general(a, b, dims, preferred_element_type=F32)


def _mmb(a, b, dims=NN):
    return _mm(a.astype(BF16), b.astype(BF16), dims)


def _split2(a):
    hi = a.astype(BF16)
    return hi, (a - hi.astype(F32)).astype(BF16)


def _split3(a):
    hi = a.astype(BF16)
    r = a - hi.astype(F32)
    mid = r.astype(BF16)
    return hi, mid, (r - mid.astype(F32)).astype(BF16)


def _mm3(a, b, dims=NN):
    ah, al = _split2(a)
    bh, bl = _split2(b)
    return _mm(ah, bh, dims) + (_mm(ah, bl, dims) + _mm(al, bh, dims))


def _mmx(a, b_exact, dims=NN):
    ah, am, al = _split3(a)
    return _mm(ah, b_exact, dims) + (_mm(am, b_exact, dims) + _mm(al, b_exact, dims))


def _xmm(a_exact, b, dims=NN):
    bh, bm, bl = _split3(b)
    return _mm(a_exact, bh, dims) + (_mm(a_exact, bm, dims) + _mm(a_exact, bl, dims))


def _iota(shape, dim):
    return lax.broadcasted_iota(I32, shape, dim)


def _head_mask(rows, cols):
    return (_iota((rows, cols), 0) >> HEAD_SHIFT) == (_iota((rows, cols), 1) >> HEAD_SHIFT)


def _head_mask_bf(rows, cols):
    return jnp.where(_head_mask(rows, cols), 1.0, 0.0).astype(BF16)


def _headsum(x, mbd_bf):
    return _mmx(x, mbd_bf)


def _bd(x, mask_bf):
    return jnp.concatenate([x.astype(BF16)] * N_HEADS, axis=1) * mask_bf


def _tri_inverse(a_all, mask_bf):
    def times(x, y):
        n = x.shape[1]
        xh, xl = _split2(x)
        yh, yl = _split2(y)
        top = _mm(jnp.concatenate([xh, xl], axis=1), _bd(yh, mask_bf), BNN)
        return (top[:, 0:n] + top[:, n:2 * n]) + _mm(xh, _bd(yl, mask_bf), BNN)

    t = _iota(a_all.shape, 1)
    j = _iota(a_all.shape, 2) & (CH - 1)
    p = jnp.where(t == j, 1.0, 0.0) + a_all
    a = times(a_all, a_all)
    levels = CH.bit_length() - 2
    for lvl in range(levels):
        if lvl + 1 < levels:
            both = times(jnp.concatenate([a, p], axis=1), a)
            a = both[:, 0:CH]
            p = p + both[:, CH:2 * CH]
        else:
            p = p + times(p, a)
    return p


def _scan_dist(ncols):
    shape = (2 * SCAN_B, CH, ncols)
    t = _iota(shape, 1)
    j = _iota(shape, 2) & (CH - 1)
    return jnp.where(_iota(shape, 0) >= SCAN_B, j - t, t - j)


def _tri_ones():
    return jnp.where(_scan_dist(CH) >= 0, 1.0, 0.0).astype(BF16)


def _chains(fwd_ref, rev_ref):
    return jnp.concatenate([fwd_ref[...], rev_ref[...]], axis=0)


def _rms_modulate(x, g, shift, scale):
    y = x * lax.rsqrt(jnp.mean(x * x, axis=-1, keepdims=True) + RMS_EPS) * g
    return y * (1.0 + scale) + shift


def _softplus(x):
    return jnp.maximum(x, 0.0) + jnp.log(1.0 + jnp.exp(-jnp.abs(x)))


def _silu(x):
    return x * jax.nn.sigmoid(x)


def _gelu_tanh(x):
    return 0.5 * x * (1.0 + jnp.tanh(np.sqrt(2.0 / np.pi).astype(np.float32) * (x + 0.044715 * (x * x * x))))


def _shift_rows(p, prev_row, next_row):
    n = p.shape[0]
    r = _iota((n, 1), 0)
    up = jnp.where(r == 0, prev_row, pltpu.roll(p, 1, 0))
    dn = jnp.where(r == n - 1, next_row, pltpu.roll(p, n - 1, 0))
    return up, dn


def _seq_edges(i, n_ctx_tiles, n_tiles):
    first = jnp.logical_or(i == 0, i == n_ctx_tiles)
    last = jnp.logical_or(i == n_ctx_tiles - 1, i == n_tiles - 1)
    return first, last


def _fill_conv_scratch(scr, cur, prev8, next8, first, last):
    n = cur.shape[0]
    scr[0:HALO, :] = jnp.where(first, 0.0, prev8)
    scr[HALO:HALO + n, :] = cur
    scr[HALO + n:2 * HALO + n, :] = jnp.where(last, 0.0, next8)


def _conv4(scr, w, n):
    return (scr[HALO - 2:HALO - 2 + n, :] * w[0:1, :] + scr[HALO - 1:HALO - 1 + n, :] * w[1:2, :]
            + scr[HALO:HALO + n, :] * w[2:3, :] + scr[HALO + 1:HALO + 1 + n, :] * w[3:4, :])


def _ada_kernel(c_ref, w_ref, b_ref, o_ref):
    c = c_ref[...]
    o_ref[...] = _mm3(_silu(c), w_ref[...]) + b_ref[...]


def _ada_call(cvec, w, b):
    n = w.shape[1]
    tn = 768
    return pl.pallas_call(
        _ada_kernel,
        grid=(n // tn,),
        in_specs=[pl.BlockSpec((8, D_MODEL), lambda j: (0, 0)),
                  pl.BlockSpec((D_MODEL, tn), lambda j: (0, j)),
                  pl.BlockSpec((1, tn), lambda j: (0, j))],
        out_specs=pl.BlockSpec((8, tn), lambda j: (0, j)),
        out_shape=jax.ShapeDtypeStruct((8, n), F32),
        compiler_params=_cparams("arbitrary"),
        name="ada_mod",
    )(cvec, w, b)


IN_COLS = (1024, 1024, 1024, 512, 128)


def _inproj_kernel(x_ref, g_ref, mod_ref, w_ref, rw_ref, ret_ref, gdn_ref, lru_ref, ab_ref):
    h = _rms_modulate(x_ref[...], g_ref[...], mod_ref[0:1, :], mod_ref[1:2, :])
    p = _mm(h.astype(BF16), w_ref[...])
    o = 0
    for ref, n in zip((rw_ref, ret_ref, gdn_ref, lru_ref, ab_ref), IN_COLS):
        ref[...] = p[:, o:o + n]
        o += n


def _mod_spec(nc):
    return pl.BlockSpec((None, None, 6, D_MODEL), lambda b, i: (b, jnp.minimum(i // nc, 1), 0, 0))


def _inproj_call(xs, gain, mod, w, nc):
    B, S, _ = xs.shape
    n = w.shape[1]
    row = lambda width: pl.BlockSpec((None, TM, width), lambda b, i: (b, i, 0))
    return pl.pallas_call(
        _inproj_kernel,
        grid=(B, S // TM),
        in_specs=[row(D_MODEL), pl.BlockSpec((1, D_MODEL), lambda b, i: (0, 0)), _mod_spec(nc),
                  pl.BlockSpec((D_MODEL, n), lambda b, i: (0, 0))],
        out_specs=[row(c) for c in IN_COLS],
        out_shape=[jax.ShapeDtypeStruct((B, S, c), F32) for c in IN_COLS],
        compiler_params=_cparams("arbitrary", "arbitrary"),
        name="in_proj",
    )(xs, gain, mod, w)


def _rwkv_prep_kernel(nc, nt, has_vres, *refs):
    if has_vres:
        (p_ref, pp_ref, pn_ref, vf_ref, mu_ref, w0_ref, w2_ref, a0_ref, a2_ref, g2_ref, kkw_ref, ka_ref, rk_ref,
         v0_ref, v2_ref, sh_ref, dp_ref, bonus_ref, gate_ref) = refs
    else:
        (p_ref, pp_ref, pn_ref, mu_ref, w0_ref, w2_ref, a0_ref, a2_ref, g2_ref, kkw_ref, ka_ref, rk_ref,
         sh_ref, dp_ref, bonus_ref, gate_ref) = refs
    i = pl.program_id(1)
    first, last = _seq_edges(i, nc, nt)
    p = p_ref[...]
    prev_row = jnp.where(first, 0.0, pp_ref[HALO - 1:HALO, :])
    next_row = jnp.where(last, 0.0, pn_ref[0:1, :])
    up, dn = _shift_rows(p, prev_row, next_row)
    ps = p + (0.5 * (up + dn) - p) * mu_ref[...]
    mbd = _head_mask(GROUP_W, GROUP_W).astype(BF16)

    r = ps[:, 0:256]
    k = ps[:, 256:512]
    v = ps[:, 512:768]
    x1 = ps[:, 768:896]
    x2 = ps[:, 896:1024]
    z = w0_ref[...] + _mmb(jnp.tanh(x1), w2_ref[...])
    lw = -np.exp(-0.5).astype(np.float32) * jax.nn.sigmoid(z)
    a = jax.nn.sigmoid(a0_ref[...] + _mmb(x1, a2_ref[...]))
    gate_ref[...] = _mmb(jax.nn.sigmoid(x2), g2_ref[...])
    kk = k * kkw_ref[...]
    kk = kk * lax.rsqrt(_headsum(kk * kk, mbd) + 1e-6)
    if has_vres:
        v = v + (vf_ref[...] - v) * jax.nn.sigmoid(v0_ref[...] + _mmb(x2, v2_ref[...]))
    sh_ref[:, 0:256] = r
    sh_ref[:, 256:512] = kk
    sh_ref[:, 512:768] = v
    ksum = jnp.zeros_like(k)
    for d in range(2):
        a_d = a[:, d * 256:(d + 1) * 256]
        kd = k * (1.0 + (a_d - 1.0) * ka_ref[...])
        ksum = ksum + kd
        dp_ref[d, :, 0:256] = lw[:, d * 256:(d + 1) * 256]
        dp_ref[d, :, 256:512] = kd
        dp_ref[d, :, 512:768] = kk * a_d
    bonus_ref[...] = _headsum(r * ksum * rk_ref[...], mbd) * v


def _halo_specs(width, nt):
    per = TM // HALO
    cur = pl.BlockSpec((None, TM, width), lambda b, i: (b, i, 0))
    prev = pl.BlockSpec((None, HALO, width), lambda b, i: (b, jnp.maximum(i * per - 1, 0), 0))
    nxt = pl.BlockSpec((None, HALO, width), lambda b, i: (b, jnp.minimum((i + 1) * per, nt * per - 1), 0))
    return cur, prev, nxt


def _const_spec(a):
    nd = a.ndim
    return pl.BlockSpec(a.shape, lambda b, i: (0,) * nd)


def _rwkv_prep_call(p_rw, vfirst_pack, prm, nc):
    B, S, _ = p_rw.shape
    nt = S // TM
    has_vres = vfirst_pack is not None
    cur, prev, nxt = _halo_specs(1024, nt)
    ins = [p_rw, p_rw, p_rw]
    specs = [cur, prev, nxt]
    if has_vres:
        ins.append(vfirst_pack)
        specs.append(pl.BlockSpec((None, TM, 256), lambda b, i: (b, i, 2)))
    names = ["mu", "w0", "w2", "a0", "a2", "g2", "kk", "ka", "rk"] + (["v0", "v2"] if has_vres else [])
    for nme in names:
        ins.append(prm[nme])
        specs.append(_const_spec(prm[nme]))
    row = lambda w: pl.BlockSpec((None, TM, w), lambda b, i: (b, i, 0))
    return pl.pallas_call(
        functools.partial(_rwkv_prep_kernel, nc, nt, has_vres),
        grid=(B, nt),
        in_specs=specs,
        out_specs=[row(768), pl.BlockSpec((None, 2, TM, 768), lambda b, i: (b, 0, i, 0)), row(256), row(256)],
        out_shape=[jax.ShapeDtypeStruct((B, S, 768), F32), jax.ShapeDtypeStruct((B, 2, S, 768), F32),
                   jax.ShapeDtypeStruct((B, S, 256), F32), jax.ShapeDtypeStruct((B, S, 256), F32)],
        compiler_params=_cparams("arbitrary", "arbitrary"),
        name="rwkv_prep",
    )(*ins)


def _rwkv_chains(sh, dp, s_ref, mask, mask_bf):
    r, kk, v = sh[:, :, 0:256], sh[:, :, 256:512], sh[:, :, 512:768]
    lw, kd, bb = dp[:, :, 0:256], dp[:, :, 256:512], dp[:, :, 512:768]
    g = _xmm(_tri_ones(), lw, BNN)
    g_tot = jnp.sum(lw, axis=1, keepdims=True)
    e_tot = jnp.exp(g_tot)
    eng = jnp.exp(-g)
    ab = -kk * jnp.exp(g - lw)
    bbar = bb * eng
    kbar = kd * eng
    rbar = r * jnp.exp(g)
    dist = _scan_dist(N_HEADS * CH)
    strict, incl = dist > 0, dist >= 0
    abb, rbb, vb = ab.astype(BF16), rbar.astype(BF16), v.astype(BF16)
    sc = _mm(jnp.concatenate([abb, rbb], axis=1),
             jnp.concatenate([_bd(bbar, mask_bf), _bd(kbar, mask_bf)], axis=1), BNT)
    a_ab = jnp.where(strict, sc[:, 0:CH, 0:256], 0.0)
    a_ak = jnp.where(strict, sc[:, 0:CH, 256:512], 0.0)
    a_rb = jnp.where(incl, sc[:, CH:2 * CH, 0:256], 0.0)
    a_rk = jnp.where(incl, sc[:, CH:2 * CH, 256:512], 0.0)
    tb = _tri_inverse(a_ab, mask_bf).astype(BF16)
    vbd = _bd(vb, mask_bf)
    av = _mm(jnp.concatenate([a_ak, a_rk], axis=1).astype(BF16), vbd, BNN)
    rhs, y0 = av[:, 0:CH], av[:, CH:2 * CH]
    wt = _mm(tb, _bd(abb, mask_bf), BNN)
    u0 = _mm(tb, _bd(rhs, mask_bf), BNN)
    s = s_ref[...]
    ws = _mm(jnp.concatenate([wt.astype(BF16), rbb], axis=1), s.astype(BF16), BNT)
    u = ws[:, 0:CH] + u0
    ub = u.astype(BF16)
    y = ws[:, CH:2 * CH] + _mm(a_rb.astype(BF16), _bd(ub, mask_bf), BNN) + y0
    upd = _mm(jnp.concatenate([ub, vb], axis=1),
              jnp.concatenate([(bbar * e_tot).astype(BF16), (kbar * e_tot).astype(BF16)], axis=1), BTN)
    s_ref[...] = s * e_tot + jnp.where(mask, upd, 0.0)
    return y


def _rwkv_scan_kernel(sh0_ref, sh1_ref, dp0_ref, dp1_ref, y0_ref, y1_ref, s_ref):
    @pl.when(pl.program_id(1) == 0)
    def _():
        s_ref[...] = jnp.zeros_like(s_ref)

    y = _rwkv_chains(_chains(sh0_ref, sh1_ref), _chains(dp0_ref, dp1_ref), s_ref,
                     _head_mask(GROUP_W, GROUP_W), _head_mask_bf(GROUP_W, GROUP_W))
    y0_ref[...] = y[0:SCAN_B]
    y1_ref[...] = y[SCAN_B:2 * SCAN_B]


def _rev_idx(n, n_ctx, n_all):
    return jnp.where(n < n_ctx, n_ctx - 1 - n, n_all + n_ctx - 1 - n)


def _scan_specs(width_sh, width_dp, ncc, nch):
    fwd = lambda b, n: (b, n, 0)
    rev = lambda b, n: (b, _rev_idx(n, ncc, nch), 0)
    fwd_d = lambda b, n: (b, 0, n, 0)
    rev_d = lambda b, n: (b, 1, _rev_idx(n, ncc, nch), 0)
    return ([pl.BlockSpec((SCAN_B, CH, width_sh), fwd), pl.BlockSpec((SCAN_B, CH, width_sh), rev),
             pl.BlockSpec((SCAN_B, None, CH, width_dp), fwd_d), pl.BlockSpec((SCAN_B, None, CH, width_dp), rev_d)],
            [pl.BlockSpec((SCAN_B, CH, GROUP_W), fwd), pl.BlockSpec((SCAN_B, CH, GROUP_W), rev)])


def _state_scratch():
    return [pltpu.VMEM((2 * SCAN_B, GROUP_W, GROUP_W), F32)]


def _rwkv_scan_call(sh, dp, ncc):
    B, S, _ = sh.shape
    nch = S // CH
    in_specs, out_specs = _scan_specs(768, 768, ncc, nch)
    return pl.pallas_call(
        _rwkv_scan_kernel,
        grid=(B // SCAN_B, nch),
        in_specs=in_specs,
        out_specs=out_specs,
        out_shape=[jax.ShapeDtypeStruct((B, S, GROUP_W), F32)] * 2,
        scratch_shapes=_state_scratch(),
        compiler_params=_cparams("arbitrary", "arbitrary"),
        name="rwkv_scan",
    )(sh, sh, dp, dp)


def _per_dir(fwd, rev):
    f = jnp.broadcast_to(fwd[None], (SCAN_B,) + fwd.shape)
    r = jnp.broadcast_to(rev[None], (SCAN_B,) + rev.shape)
    return jnp.concatenate([f, r], axis=0)


def _ret_chains(p, cs, lg, r_ref, mask, mask_bf):
    nc = 2 * SCAN_B
    p2 = p.reshape(nc * CH, 1024)
    cs2 = cs.reshape(nc * CH, 512)
    lane = _iota((nc * CH, GROUP_W), 1)
    first_half = (lane & (HEAD_DIM - 1)) < HEAD_DIM // 2
    cos, sin = cs2[:, 0:256], cs2[:, 256:512]

    def rope(t):
        swapped = jnp.where(first_half, pltpu.roll(t, GROUP_W - HEAD_DIM // 2, 1), pltpu.roll(t, HEAD_DIM // 2, 1))
        return (t * cos + swapped * sin).reshape(nc, CH, GROUP_W)

    q = rope(p2[:, 0:256])
    k = rope(p2[:, 256:512]) * (HEAD_DIM ** -0.5)
    v = p[:, :, 512:768]
    dist = _scan_dist(N_HEADS * CH)
    decay = jnp.where(dist >= 0, jnp.exp(jnp.maximum(dist, 0).astype(F32) * lg), 0.0)
    shape = (nc, CH, GROUP_W)
    tpos = _iota(shape, 1)
    done = jnp.where(_iota(shape, 0) >= SCAN_B, CH - 1 - tpos, tpos)
    xi = jnp.exp((done + 1).astype(F32) * lg)
    zeta = jnp.exp((CH - 1 - done).astype(F32) * lg)
    g_c = jnp.exp(float(CH) * lg)
    vb = v.astype(BF16)
    s = _mm(q.astype(BF16), _bd(k, mask_bf), BNT) * decay
    rr = r_ref[...]
    o = _mm(s.astype(BF16), _bd(vb, mask_bf), BNN) + _mm((q * xi).astype(BF16), rr.astype(BF16), BNN)
    r_ref[...] = rr * g_c + jnp.where(mask, _mm((k * zeta).astype(BF16), vb, BTN), 0.0)
    return o


def _ret_kernel(p0_ref, p1_ref, cs0_ref, cs1_ref, lg_ref, o0_ref, o1_ref, r_ref):
    @pl.when(pl.program_id(1) == 0)
    def _():
        r_ref[...] = jnp.zeros_like(r_ref)

    o = _ret_chains(_chains(p0_ref, p1_ref), _per_dir(cs0_ref[...], cs1_ref[...]),
                    _per_dir(lg_ref[0:1, :], lg_ref[1:2, :]), r_ref,
                    _head_mask(GROUP_W, GROUP_W), _head_mask_bf(GROUP_W, GROUP_W))
    o0_ref[...] = o[0:SCAN_B]
    o1_ref[...] = o[SCAN_B:2 * SCAN_B]


def _ret_call(p_ret, cs, lg, ncc):
    B, S, _ = p_ret.shape
    nch = S // CH
    fwd = lambda b, n: (b, n, 0)
    rev = lambda b, n: (b, _rev_idx(n, ncc, nch), 0)
    fwd2 = lambda b, n: (n, 0)
    rev2 = lambda b, n: (_rev_idx(n, ncc, nch), 0)
    return pl.pallas_call(
        _ret_kernel,
        grid=(B // SCAN_B, nch),
        in_specs=[pl.BlockSpec((SCAN_B, CH, 1024), fwd), pl.BlockSpec((SCAN_B, CH, 1024), rev),
                  pl.BlockSpec((CH, 512), fwd2), pl.BlockSpec((CH, 512), rev2),
                  pl.BlockSpec((2, GROUP_W), lambda b, n: (0, 0))],
        out_specs=[pl.BlockSpec((SCAN_B, CH, GROUP_W), fwd), pl.BlockSpec((SCAN_B, CH, GROUP_W), rev)],
        out_shape=[jax.ShapeDtypeStruct((B, S, GROUP_W), F32)] * 2,
        scratch_shapes=_state_scratch(),
        compiler_params=_cparams("arbitrary", "arbitrary"),
        name="retention",
    )(p_ret, p_ret, cs, cs, lg)


def _gdn_prep_kernel(nc, nt, p_ref, pp_ref, pn_ref, ab_ref, cw_ref, alog_ref, dtb_ref, e_ref, sh_ref, dp_ref, scr):
    i = pl.program_id(1)
    first, last = _seq_edges(i, nc, nt)
    _fill_conv_scratch(scr, p_ref[:, 0:768], pp_ref[:, 0:768], pn_ref[:, 0:768], first, last)
    qkv = _silu(_conv4(scr, cw_ref[...], TM))
    mbd = _head_mask(GROUP_W, GROUP_W).astype(BF16)
    q, k = qkv[:, 0:256], qkv[:, 256:512]
    sh_ref[:, 0:256] = q * lax.rsqrt(_headsum(q * q, mbd) + 1e-6) * (HEAD_DIM ** -0.5)
    sh_ref[:, 256:512] = k * lax.rsqrt(_headsum(k * k, mbd) + 1e-6)
    sh_ref[:, 512:768] = qkv[:, 512:768]
    ab = ab_ref[...]
    ld = -jnp.exp(alog_ref[...]) * _softplus(ab + dtb_ref[...])
    vec = jnp.where(_iota(ab.shape, 1) < 2 * N_HEADS, ld, jax.nn.sigmoid(ab))
    ex = _mmx(vec, e_ref[...])
    for d in range(2):
        dp_ref[d, :, 0:256] = ex[:, d * 256:(d + 1) * 256]
        dp_ref[d, :, 256:512] = ex[:, 512 + d * 256:512 + (d + 1) * 256]


def _gdn_prep_call(p_gdn, p_ab, prm, nc):
    B, S, _ = p_gdn.shape
    nt = S // TM
    cur, prev, nxt = _halo_specs(1024, nt)
    consts = [prm["conv_w"], prm["alog"], prm["dtb"], prm["expand"]]
    row = lambda w: pl.BlockSpec((None, TM, w), lambda b, i: (b, i, 0))
    return pl.pallas_call(
        functools.partial(_gdn_prep_kernel, nc, nt),
        grid=(B, nt),
        in_specs=[cur, prev, nxt, row(128)] + [_const_spec(a) for a in consts],
        out_specs=[row(768), pl.BlockSpec((None, 2, TM, 512), lambda b, i: (b, 0, i, 0))],
        out_shape=[jax.ShapeDtypeStruct((B, S, 768), F32), jax.ShapeDtypeStruct((B, 2, S, 512), F32)],
        scratch_shapes=[pltpu.VMEM((TM + 2 * HALO, 768), F32)],
        compiler_params=_cparams("arbitrary", "arbitrary"),
        name="gdn_prep",
    )(p_gdn, p_gdn, p_gdn, p_ab, *consts)


def _gdn_chains(sh, dp, s_ref, mask, mask_bf):
    q, k, v = sh[:, :, 0:256], sh[:, :, 256:512], sh[:, :, 512:768]
    ld, beta = dp[:, :, 0:256], dp[:, :, 256:512]
    gc = _xmm(_tri_ones(), ld, BNN)
    g_tot = jnp.sum(ld, axis=1, keepdims=True)
    dist = _scan_dist(N_HEADS * CH)
    gc_cols = jnp.sum(jnp.where(dist == 0, gc, 0.0), axis=1, keepdims=True)
    gam = jnp.where(dist >= 0, jnp.exp(jnp.minimum(gc - gc_cols, 0.0)), 0.0)
    kb = k * beta
    kk = _mm(jnp.concatenate([kb, q], axis=1).astype(BF16), _bd(k, mask_bf), BNT)
    a_mat = jnp.where(dist > 0, kk[:, 0:CH] * gam, 0.0)
    attn = kk[:, CH:2 * CH] * gam
    tb = _tri_inverse(-a_mat, mask_bf).astype(BF16)
    egc = jnp.exp(gc)
    u = _mm(tb, _bd(v * beta, mask_bf), BNN)
    w = _mm(tb, _bd(kb * egc, mask_bf), BNN)
    s = s_ref[...]
    ws = _mm(jnp.concatenate([w, q * egc], axis=1).astype(BF16), s.astype(BF16), BNN)
    v_new = u - ws[:, 0:CH]
    vnb = v_new.astype(BF16)
    o = ws[:, CH:2 * CH] + _mm(attn.astype(BF16), _bd(vnb, mask_bf), BNN)
    upd = _mm((k * jnp.exp(g_tot - gc)).astype(BF16), vnb, BTN)
    s_ref[...] = s * jnp.exp(g_tot) + jnp.where(mask, upd, 0.0)
    return o


def _gdn_scan_kernel(sh0_ref, sh1_ref, dp0_ref, dp1_ref, o0_ref, o1_ref, s_ref):
    @pl.when(pl.program_id(1) == 0)
    def _():
        s_ref[...] = jnp.zeros_like(s_ref)

    o = _gdn_chains(_chains(sh0_ref, sh1_ref), _chains(dp0_ref, dp1_ref), s_ref,
                    _head_mask(GROUP_W, GROUP_W), _head_mask_bf(GROUP_W, GROUP_W))
    o0_ref[...] = o[0:SCAN_B]
    o1_ref[...] = o[SCAN_B:2 * SCAN_B]


def _gdn_scan_call(sh, dp, ncc):
    B, S, _ = sh.shape
    nch = S // CH
    in_specs, out_specs = _scan_specs(768, 512, ncc, nch)
    return pl.pallas_call(
        _gdn_scan_kernel,
        grid=(B // SCAN_B, nch),
        in_specs=in_specs,
        out_specs=out_specs,
        out_shape=[jax.ShapeDtypeStruct((B, S, GROUP_W), F32)] * 2,
        scratch_shapes=_state_scratch(),
        compiler_params=_cparams("arbitrary", "arbitrary"),
        name="gdn_scan",
    )(sh, sh, dp, dp)


def _lru_tile(d, rev, first, last, p_ref, pp_ref, pn_ref, cw_ref, cb_ref, wa_ref, ba_ref, wx_ref, bx_ref, lam_ref,
              h_ref, carry_ref, scr):
    _fill_conv_scratch(scr, p_ref[:, 0:256], pp_ref[:, 0:256], pn_ref[:, 0:256], first, last)
    xm = _conv4(scr, cw_ref[...], TM) + cb_ref[...]
    xb = xm.astype(BF16)
    r = jax.nn.sigmoid(_mm(xb, wa_ref[d]) + ba_ref[d:d + 1, :])
    ig = jax.nn.sigmoid(_mm(xb, wx_ref[d]) + bx_ref[d:d + 1, :])
    log_a = -LRU_C * _softplus(-lam_ref[d:d + 1, :]) * r
    a = jnp.exp(log_a)
    th = jnp.tanh(log_a)
    b = jnp.sqrt(-2.0 * th / (1.0 - th)) * ig * xm
    row = _iota((TM, 1), 0)
    s = 1
    while s < TM:
        if rev:
            ok = row < TM - s
            a_n, b_n = pltpu.roll(a, TM - s, 0), pltpu.roll(b, TM - s, 0)
        else:
            ok = row >= s
            a_n, b_n = pltpu.roll(a, s, 0), pltpu.roll(b, s, 0)
        b = jnp.where(ok, b + a * b_n, b)
        a = jnp.where(ok, a * a_n, a)
        s *= 2
    h = b + a * carry_ref[0:1, :]
    h_ref[...] = h
    end = h[0:1, :] if rev else h[TM - 1:TM, :]
    carry_ref[...] = jnp.broadcast_to(end, carry_ref.shape)


def _lru_kernel(nc, nt, p0, pp0, pn0, p1, pp1, pn1, cw, cb, wa, ba, wx, bx, lam, h0_ref, h1_ref, c0, c1, scr):
    i = pl.program_id(1)

    @pl.when(i == 0)
    def _():
        c0[...] = jnp.zeros_like(c0)
        c1[...] = jnp.zeros_like(c1)

    first, last = _seq_edges(i, nc, nt)
    _lru_tile(0, False, first, last, p0, pp0, pn0, cw, cb, wa, ba, wx, bx, lam, h0_ref, c0, scr)
    j = _rev_idx(i, nc, nt)
    first, last = _seq_edges(j, nc, nt)
    _lru_tile(1, True, first, last, p1, pp1, pn1, cw, cb, wa, ba, wx, bx, lam, h1_ref, c1, scr)


def _lru_call(p_lru, prm, nc):
    B, S, _ = p_lru.shape
    nt = S // TM
    per = TM // HALO
    rj = lambda i: _rev_idx(i, nc, nt)
    cur, prev, nxt = _halo_specs(512, nt)
    cur_r = pl.BlockSpec((None, TM, 512), lambda b, i: (b, rj(i), 0))
    prev_r = pl.BlockSpec((None, HALO, 512), lambda b, i: (b, jnp.maximum(rj(i) * per - 1, 0), 0))
    nxt_r = pl.BlockSpec((None, HALO, 512), lambda b, i: (b, jnp.minimum((rj(i) + 1) * per, nt * per - 1), 0))
    consts = [prm[k] for k in ("conv_w", "conv_b", "wa", "ba", "wx", "bx", "lam")]
    return pl.pallas_call(
        functools.partial(_lru_kernel, nc, nt),
        grid=(B, nt),
        in_specs=[cur, prev, nxt, cur_r, prev_r, nxt_r] + [_const_spec(a) for a in consts],
        out_specs=[pl.BlockSpec((None, TM, GROUP_W), lambda b, i: (b, i, 0)),
                   pl.BlockSpec((None, TM, GROUP_W), lambda b, i: (b, rj(i), 0))],
        out_shape=[jax.ShapeDtypeStruct((B, S, GROUP_W), F32)] * 2,
        scratch_shapes=[pltpu.VMEM((8, GROUP_W), F32), pltpu.VMEM((8, GROUP_W), F32),
                        pltpu.VMEM((TM + 2 * HALO, GROUP_W), F32)],
        compiler_params=_cparams("arbitrary", "arbitrary"),
        name="rglru",
    )(p_lru, p_lru, p_lru, p_lru, p_lru, p_lru, *consts)


def _mix_out_kernel(y0, y1, bonus, grw, ro0, ro1, gret, go0, go1, ggdn, lh0, lh1, glru, x_ref, mod_ref, w_ref,
                    lnw, lnb, gnw, o_ref):
    mbd = _head_mask(GROUP_W, GROUP_W).astype(BF16)
    inv = 1.0 / HEAD_DIM

    def head_rms(o):
        return o * lax.rsqrt(_headsum(o * o, mbd) * inv + RMS_EPS)

    y = y0[...] + y1[...]
    mu = _headsum(y, mbd) * inv
    dy = y - mu
    var = _headsum(dy * dy, mbd) * inv
    a = (dy * lax.rsqrt(var + GN_EPS) * lnw[...] + lnb[...] + bonus[...]) * grw[...]
    b = head_rms(ro0[...] + ro1[...]) * _silu(gret[...])
    g = head_rms(go0[...] + go1[...]) * gnw[...] * _silu(ggdn[...])
    r = (lh0[...] + lh1[...]) * _gelu_tanh(glru[...])
    w = w_ref[...]
    mix = (_mm(a.astype(BF16), w[0:256]) + _mm(b.astype(BF16), w[256:512])
           + _mm(g.astype(BF16), w[512:768]) + _mm(r.astype(BF16), w[768:1024]))
    o_ref[...] = x_ref[...] + mod_ref[2:3, :] * mix


def _mix_out_call(ys, bonus, grw, ros, p_ret, gos, p_gdn, lhs, p_lru, xs, mod, w_out, lnw, lnb, gnw, nc):
    B, S, _ = xs.shape
    row = lambda w, c=0: pl.BlockSpec((None, TM, w), lambda b, i: (b, i, c))
    g = row(GROUP_W)
    consts = [lnw, lnb, gnw]
    return pl.pallas_call(
        _mix_out_kernel,
        grid=(B, S // TM),
        in_specs=[g, g, g, g, g, g, row(GROUP_W, 3), g, g, row(GROUP_W, 3), g, g, row(GROUP_W, 1),
                  row(D_MODEL), _mod_spec(nc), pl.BlockSpec((D_MODEL, D_MODEL), lambda b, i: (0, 0))]
                 + [_const_spec(a) for a in consts],
        out_specs=row(D_MODEL),
        out_shape=jax.ShapeDtypeStruct((B, S, D_MODEL), F32),
        compiler_params=_cparams("arbitrary", "arbitrary"),
        name="mix_out",
    )(ys[0], ys[1], bonus, grw, ros[0], ros[1], p_ret, gos[0], gos[1], p_gdn, lhs[0], lhs[1], p_lru,
      xs, mod, w_out, *consts)


def _moe_pre_kernel(x_ref, g_ref, mod_ref, rw_ref, rb_ref, sg_ref, su_ref, sd_ref, h_ref, idx_ref, gate_ref, sh_ref):
    h = _rms_modulate(x_ref[...], g_ref[...], mod_ref[3:4, :], mod_ref[4:5, :])
    hb = h.astype(BF16)
    bits = pltpu.bitcast(hb.astype(F32), jnp.uint32)
    half = D_MODEL // 2
    h_ref[...] = (bits[:, 0:half] >> 16) | (bits[:, half:D_MODEL] & jnp.uint32(0xFFFF0000))
    scores = jax.nn.sigmoid(_mm3(h, rw_ref[...]))
    sel = scores + rb_ref[...]
    lane = _iota(scores.shape, 1)
    gates = jnp.zeros_like(scores)
    idxs = jnp.zeros(scores.shape, I32)
    total = jnp.zeros((scores.shape[0], 1), F32)
    for kk in range(TOP_K):
        m = jnp.max(sel, axis=-1, keepdims=True)
        idx = jnp.min(jnp.where(sel == m, lane, 2 * N_EXPERTS), axis=-1, keepdims=True)
        hit = lane == idx
        gk = jnp.sum(jnp.where(hit, scores, 0.0), axis=-1, keepdims=True)
        total = total + gk
        gates = jnp.where(lane == kk, gk, gates)
        idxs = jnp.where(lane == kk, idx, idxs)
        sel = jnp.where(hit, -jnp.inf, sel)
    gate_ref[...] = ROUTED_SCALE * gates / total
    idx_ref[...] = idxs
    act = _silu(_mm(hb, sg_ref[...])) * _mm(hb, su_ref[...])
    sh_ref[...] = _mm(act.astype(BF16), sd_ref[...])


def _moe_pre_call(xs, gain, mod, rw, rb, sg, su, sd, nc):
    B, S, _ = xs.shape
    row = lambda w: pl.BlockSpec((None, TM, w), lambda b, i: (b, i, 0))
    consts = [rw, rb, sg, su, sd]
    return pl.pallas_call(
        _moe_pre_kernel,
        grid=(B, S // TM),
        in_specs=[row(D_MODEL), pl.BlockSpec((1, D_MODEL), lambda b, i: (0, 0)), _mod_spec(nc)]
                 + [_const_spec(a) for a in consts],
        out_specs=[row(D_MODEL // 2), row(128), row(128), row(D_MODEL)],
        out_shape=[jax.ShapeDtypeStruct((B, S, D_MODEL // 2), jnp.uint32), jax.ShapeDtypeStruct((B, S, 128), I32),
                   jax.ShapeDtypeStruct((B, S, 128), F32), jax.ShapeDtypeStruct((B, S, D_MODEL), F32)],
        compiler_params=_cparams("arbitrary", "arbitrary"),
        name="moe_pre",
    )(xs, gain, mod, *consts)


def _row_out_copy(src, dst_hbm, sem, r, dst_row):
    return pltpu.make_async_copy(src.at[pl.ds(r, 1), :], dst_hbm.at[pl.ds(dst_row, 1), :], sem)


def _wait_block_out(yb_slot, y_hbm, sem):
    for r in range(MOE_BLOCK):
        _row_out_copy(yb_slot, y_hbm, sem, r, 0).wait()


def _start_block_out(yb_slot, y_hbm, sem, dst_ref):
    for r in range(MOE_BLOCK):
        _row_out_copy(yb_slot, y_hbm, sem, r, dst_ref[0, 0, r]).start()


def _expert_kernel(blk_e_ref, tok_ref, dst_ref, dstp_ref, hp_hbm, wg_ref, wu_ref, wd_ref, y_hbm,
                   tab, xb, yb, tsem, ssem):
    i = pl.program_id(0)
    nb = pl.num_programs(0)
    slot = lax.rem(i, 2)
    other = 1 - slot

    @pl.when(i == 0)
    def _():
        load = pltpu.make_async_copy(hp_hbm, tab, tsem)
        load.start()
        yb[1] = jnp.zeros((MOE_BLOCK, D_MODEL), F32)
        load.wait()

    _start_block_out(yb.at[other], y_hbm, ssem.at[other], dstp_ref)
    for r in range(MOE_BLOCK):
        xb[pl.ds(r, 1), :] = tab[pl.ds(tok_ref[0, 0, r], 1), :]
    u = xb[...]
    lo = pltpu.bitcast(u << 16, F32).astype(BF16)
    hi = pltpu.bitcast(u & jnp.uint32(0xFFFF0000), F32).astype(BF16)
    x = jnp.concatenate([lo, hi], axis=1)
    act = _silu(_mm(x, wg_ref[...])) * _mm(x, wu_ref[...])
    y = _mm(act.astype(BF16), wd_ref[...])

    @pl.when(i >= 1)
    def _():
        _wait_block_out(yb.at[slot], y_hbm, ssem.at[slot])

    yb[slot] = y

    @pl.when(i == nb - 1)
    def _():
        _start_block_out(yb.at[slot], y_hbm, ssem.at[slot], dst_ref)
        _wait_block_out(yb.at[other], y_hbm, ssem.at[other])
        _wait_block_out(yb.at[slot], y_hbm, ssem.at[slot])


def _expert_call(blk_e, slot_tok, slot_dst, dst_prev, hp_flat, wg, wu, wd, n_rows):
    nb = blk_e.shape[0]
    ff = wg.shape[2]
    n_tok, half = hp_flat.shape
    idx_spec = pl.BlockSpec((1, 1, MOE_BLOCK), lambda i, be: (i, 0, 0), memory_space=pltpu.SMEM)
    grid_spec = pltpu.PrefetchScalarGridSpec(
        num_scalar_prefetch=1,
        grid=(nb,),
        in_specs=[idx_spec, idx_spec, idx_spec,
                  pl.BlockSpec(memory_space=pl.ANY),
                  pl.BlockSpec((None, D_MODEL, ff), lambda i, be: (be[i], 0, 0)),
                  pl.BlockSpec((None, D_MODEL, ff), lambda i, be: (be[i], 0, 0)),
                  pl.BlockSpec((None, ff, D_MODEL), lambda i, be: (be[i], 0, 0))],
        out_specs=pl.BlockSpec(memory_space=pl.ANY),
        scratch_shapes=[pltpu.VMEM((n_tok, half), jnp.uint32), pltpu.VMEM((MOE_BLOCK, half), jnp.uint32),
                        pltpu.VMEM((2, MOE_BLOCK, D_MODEL), F32),
                        pltpu.SemaphoreType.DMA(()), pltpu.SemaphoreType.DMA((2,))],
    )
    return pl.pallas_call(
        _expert_kernel,
        grid_spec=grid_spec,
        out_shape=jax.ShapeDtypeStruct((n_rows, D_MODEL), F32),
        compiler_params=pltpu.CompilerParams(dimension_semantics=("arbitrary",),
                                             vmem_limit_bytes=EXPERT_VMEM_LIMIT_BYTES),
        name="moe_experts",
    )(blk_e, slot_tok, slot_dst, dst_prev, hp_flat, wg, wu, wd)


def _combined(y_ref, gate_ref, sh_ref, x_ref, mod_ref):
    gate = gate_ref[...]
    routed = jnp.zeros((COMB_T, D_MODEL), F32)
    for k in range(TOP_K):
        routed = routed + gate[:, k:k + 1] * y_ref[k * COMB_T:(k + 1) * COMB_T, :]
    return x_ref[...] + mod_ref[5:6, :] * (routed + sh_ref[...])


def _combine_kernel(y_ref, gate_ref, sh_ref, x_ref, mod_ref, o_ref):
    o_ref[...] = _combined(y_ref, gate_ref, sh_ref, x_ref, mod_ref)


def _combine_norm_kernel(y_ref, gate_ref, sh_ref, x_ref, mod_ref, g_ref, o_ref):
    x = _combined(y_ref, gate_ref, sh_ref, x_ref, mod_ref)
    o_ref[...] = x * lax.rsqrt(jnp.mean(x * x, axis=-1, keepdims=True) + RMS_EPS) * g_ref[...]


def _combine_call(y, gate, shared, xs, mod, nc, final_gain=None):
    B, S, _ = xs.shape
    per = S // COMB_T
    skip = 0 if final_gain is None else nc * TM // COMB_T
    row = lambda w: pl.BlockSpec((None, COMB_T, w), lambda b, i: (b, i + skip, 0))
    mod_spec = pl.BlockSpec((None, None, 6, D_MODEL),
                            lambda b, i: (b, jnp.minimum((i + skip) // (nc * TM // COMB_T), 1), 0, 0))
    in_specs = [pl.BlockSpec((COMB_T * TOP_K, D_MODEL), lambda b, i: (b * per + i + skip, 0)),
                row(128), row(D_MODEL), row(D_MODEL), mod_spec]
    args = [y, gate, shared, xs, mod]
    if final_gain is not None:
        in_specs.append(pl.BlockSpec((1, D_MODEL), lambda b, i: (0, 0)))
        args.append(final_gain)
    return pl.pallas_call(
        _combine_kernel if final_gain is None else _combine_norm_kernel,
        grid=(B, per - skip),
        in_specs=in_specs,
        out_specs=pl.BlockSpec((None, COMB_T, D_MODEL), lambda b, i: (b, i, 0)),
        out_shape=jax.ShapeDtypeStruct((B, S - skip * COMB_T, D_MODEL), F32),
        compiler_params=_cparams("arbitrary", "arbitrary"),
        name="moe_combine",
    )(*args)


N_RWKV, N_RET, N_GDN, N_LRU = 960, 1024, 1040, 512


def _pack_w_in(w_in_l, w_vres_l):
    o_ret = N_RWKV
    o_gdn = o_ret + N_RET
    o_lru = o_gdn + N_GDN
    z = lambda n: jnp.zeros((D_MODEL, n), F32)
    vres = z(LORA_W) if w_vres_l is None else w_vres_l
    cols = [w_in_l[:, 0:N_RWKV], vres, z(1024 - N_RWKV - LORA_W),
            w_in_l[:, o_ret:o_gdn],
            w_in_l[:, o_gdn:o_gdn + 768], w_in_l[:, o_gdn + 784:o_lru],
            w_in_l[:, o_lru:o_lru + N_LRU],
            w_in_l[:, o_gdn + 768:o_gdn + 784], z(112)]
    return jnp.concatenate(cols, axis=1).astype(BF16)


def _block_diag2(w):
    r, c = w.shape[1], w.shape[2]
    z = jnp.zeros((r, c), F32)
    return jnp.concatenate([jnp.concatenate([w[0], z], 1), jnp.concatenate([z, w[1]], 1)], 0)


def _rwkv_params(l, rw_mu, rw_w0, rw_w2, rw_a0, rw_a2, rw_g2, rw_kk, rw_ka, rw_rk, rw_v0, rw_v2):
    z64 = jnp.zeros((2 * LORA_W, 2 * GROUP_W), F32)
    prm = {
        "mu": jnp.pad(rw_mu[l], (0, 1024 - N_RWKV))[None, :],
        "w0": rw_w0[l].reshape(1, 2 * GROUP_W),
        "w2": jnp.concatenate([_block_diag2(rw_w2[l]), z64], 0),
        "a0": rw_a0[l].reshape(1, 2 * GROUP_W),
        "a2": jnp.concatenate([z64, _block_diag2(rw_a2[l])], 0),
        "g2": jnp.concatenate([rw_g2[l], jnp.zeros((64, GROUP_W), F32)], 0),
        "kk": rw_kk[l][None, :],
        "ka": rw_ka[l][None, :],
        "rk": rw_rk[l].reshape(1, GROUP_W),
    }
    if l > 0:
        prm["v0"] = rw_v0[l - 1][None, :]
        prm["v2"] = jnp.concatenate([jnp.zeros((64, GROUP_W), F32), rw_v2[l - 1],
                                     jnp.zeros((128 - 64 - LORA_W, GROUP_W), F32)], 0)
    return prm


def _head_expand():
    e = np.zeros((128, 1024), np.float32)
    for c in range(4 * N_HEADS):
        grp, h = divmod(c, N_HEADS)
        e[c, grp * GROUP_W + h * HEAD_DIM: grp * GROUP_W + (h + 1) * HEAD_DIM] = 1.0
    return jnp.asarray(e, BF16)


def _lanes_per_head(v):
    return jnp.repeat(v, HEAD_DIM, axis=-1)


def _block_diag_heads(w):
    out = jnp.zeros((GROUP_W, GROUP_W), F32)
    for h in range(N_HEADS):
        out = out.at[h * HEAD_DIM:(h + 1) * HEAD_DIM, h * HEAD_DIM:(h + 1) * HEAD_DIM].set(w[h])
    return out


def _rope_tables(tc, tx):
    t = np.arange(tx)
    rows = (t // GRID_W).astype(np.float32)
    cols = (t % GRID_W).astype(np.float32)
    nf = HEAD_DIM // 4
    inv = (ROPE_BASE ** (-np.arange(nf, dtype=np.float32) / nf)).astype(np.float32)
    ang = jnp.concatenate([jnp.asarray(rows)[:, None] * inv, jnp.asarray(cols)[:, None] * inv], -1)
    cos, sin = jnp.cos(ang), jnp.sin(ang)
    cos_h = jnp.concatenate([cos, cos], -1)
    sin_h = jnp.concatenate([-sin, sin], -1)
    cos_f = jnp.tile(cos_h, (1, N_HEADS))
    sin_f = jnp.tile(sin_h, (1, N_HEADS))
    cos_f = jnp.concatenate([jnp.ones((tc, GROUP_W), F32), cos_f], 0)
    sin_f = jnp.concatenate([jnp.zeros((tc, GROUP_W), F32), sin_f], 0)
    return jnp.concatenate([cos_f, sin_f], -1)


def _route(idx8, n_tok):
    tk = n_tok * TOP_K
    flat_e = idx8.reshape(tk)
    order = jnp.argsort(flat_e).astype(I32)
    experts = jnp.arange(N_EXPERTS, dtype=I32)
    counts = jnp.sum((flat_e[:, None] == experts[None, :]).astype(I32), axis=0)
    nblk = (counts + MOE_BLOCK - 1) // MOE_BLOCK
    blk_end = jnp.cumsum(nblk)
    blk_start = blk_end - nblk
    grp_start = jnp.cumsum(counts) - counts
    nb = tk // MOE_BLOCK + N_EXPERTS
    bi = jnp.arange(nb, dtype=I32)
    blk_e = jnp.minimum(jnp.sum((blk_end[None, :] <= bi[:, None]).astype(I32), axis=1), N_EXPERTS - 1)
    j = bi - blk_start[blk_e]
    n_valid = jnp.clip(counts[blk_e] - j * MOE_BLOCK, 0, MOE_BLOCK)
    r = jnp.arange(MOE_BLOCK, dtype=I32)
    sorted_pos = (grp_start[blk_e] + j * MOE_BLOCK)[:, None] + r[None, :]
    valid = r[None, :] < n_valid[:, None]
    flat = order[jnp.clip(sorted_pos, 0, tk - 1)]
    tok = flat // TOP_K
    choice = flat - tok * TOP_K
    dst = (tok // COMB_T) * (COMB_T * TOP_K) + choice * COMB_T + tok % COMB_T
    spare = tk + (bi % 2)[:, None] * MOE_BLOCK + r[None, :]
    slot_dst = jnp.where(valid, dst, spare).astype(I32)
    slot_tok = jnp.where(valid, tok, 0).astype(I32)
    first = (tk + MOE_BLOCK + r)[None, :].astype(I32)
    dst_prev = jnp.concatenate([first, slot_dst[:-1]], axis=0)
    return (slot_tok.reshape(nb, 1, MOE_BLOCK), slot_dst.reshape(nb, 1, MOE_BLOCK),
            dst_prev.reshape(nb, 1, MOE_BLOCK), blk_e.astype(I32), tk + 2 * MOE_BLOCK)


def kernel(x, c, ctx, c_ctx, ada_w, ada_b, norm_mix, norm_ffn, norm_final, w_in, w_vres, w_out, rw_mu, rw_w0, rw_w2, rw_a0, rw_a2, rw_g2, rw_kk, rw_ka, rw_rk, rw_ln_w, rw_ln_b, rw_v0, rw_v2, ret_lambda, gdn_conv_w, gdn_a_log, gdn_dt_bias, gdn_norm_w, lru_conv_w, lru_conv_b, lru_w_a, lru_b_a, lru_w_x, lru_b_x, lru_lambda, router_w, router_bias, exp_w_gate, exp_w_up, exp_w_down, sh_w_gate, sh_w_up, sh_w_down):
    B, tx, _ = x.shape
    tc = ctx.shape[1]
    depth = w_in.shape[0]
    assert tc % TM == 0 and tx % TM == 0 and x.shape[2] == D_MODEL and B % SCAN_B == 0
    nc = tc // TM
    ncc = tc // CH
    S = tc + tx
    xs = jnp.concatenate([ctx, x], axis=1)
    cvec = jnp.concatenate([c, c_ctx[None, :], jnp.zeros((8 - B - 1, D_MODEL), F32)], 0)
    rope = _rope_tables(tc, tx)
    expand = _head_expand()
    vfirst = None
    for l in range(depth):
        ada = _ada_call(cvec, ada_w[l], ada_b[l][None, :]).reshape(8, 6, D_MODEL)
        mod = jnp.stack([jnp.broadcast_to(ada[B][None], (B, 6, D_MODEL)), ada[:B]], axis=1)

        w_l = _pack_w_in(w_in[l], None if l == 0 else w_vres[l - 1])
        p_rw, p_ret, p_gdn, p_lru, p_ab = _inproj_call(xs, norm_mix[l][None, :], mod, w_l, nc)

        rprm = _rwkv_params(l, rw_mu, rw_w0, rw_w2, rw_a0, rw_a2, rw_g2, rw_kk, rw_ka, rw_rk, rw_v0, rw_v2)
        sh_rw, dp_rw, bonus, g_rw = _rwkv_prep_call(p_rw, vfirst, rprm, nc)
        if l == 0:
            vfirst = sh_rw
        ys = _rwkv_scan_call(sh_rw, dp_rw, ncc)

        ros = _ret_call(p_ret, rope, _lanes_per_head(-ret_lambda[l]), ncc)

        gprm = {"conv_w": gdn_conv_w[l],
                "alog": jnp.pad(gdn_a_log[l].reshape(1, 2 * N_HEADS), ((0, 0), (0, 128 - 2 * N_HEADS))),
                "dtb": jnp.pad(gdn_dt_bias[l].reshape(1, 2 * N_HEADS), ((0, 0), (0, 128 - 2 * N_HEADS))),
                "expand": expand}
        sh_g, dp_g = _gdn_prep_call(p_gdn, p_ab, gprm, nc)
        gos = _gdn_scan_call(sh_g, dp_g, ncc)

        lprm = {"conv_w": lru_conv_w[l], "conv_b": lru_conv_b[l][None, :],
                "wa": jnp.stack([_block_diag_heads(lru_w_a[l, d]) for d in range(2)]).astype(BF16),
                "ba": lru_b_a[l],
                "wx": jnp.stack([_block_diag_heads(lru_w_x[l, d]) for d in range(2)]).astype(BF16),
                "bx": lru_b_x[l], "lam": lru_lambda[l]}
        lhs = _lru_call(p_lru, lprm, nc)

        xs = _mix_out_call(ys, bonus, g_rw, ros, p_ret, gos, p_gdn, lhs, p_lru, xs, mod, w_out[l].astype(BF16),
                           rw_ln_w[l][None, :], rw_ln_b[l][None, :], jnp.tile(gdn_norm_w[l], N_HEADS)[None, :], nc)

        rw_pad = jnp.pad(router_w[l], ((0, 0), (0, 128 - N_EXPERTS)))
        rb_pad = jnp.concatenate([router_bias[l], jnp.full((128 - N_EXPERTS,), -jnp.inf, F32)])[None, :]
        h2, idx, gate, shared = _moe_pre_call(xs, norm_ffn[l][None, :], mod, rw_pad, rb_pad,
                                              sh_w_gate[l].astype(BF16), sh_w_up[l].astype(BF16),
                                              sh_w_down[l].astype(BF16), nc)
        n_tok = B * S
        slot_tok, slot_dst, dst_prev, blk_e, n_rows = _route(idx[:, :, :TOP_K], n_tok)
        y = _expert_call(blk_e, slot_tok, slot_dst, dst_prev, h2.reshape(n_tok, D_MODEL // 2), exp_w_gate[l].astype(BF16),
                         exp_w_up[l].astype(BF16), exp_w_down[l].astype(BF16), n_rows)
        xs = _combine_call(y, gate, shared, xs, mod, nc, norm_final[None, :] if l == depth - 1 else None)
    return xs
```

```python
import functools

import jax
import jax.numpy as jnp
import numpy as np
from jax import lax
from jax.experimental import pallas as pl
from jax.experimental.pallas import tpu as pltpu

F32, BF16, I32 = jnp.float32, jnp.bfloat16, jnp.int32

D_MODEL = 1024
GROUP_W = 256
HEAD_DIM = 64
N_HEADS = GROUP_W // HEAD_DIM
HEAD_SHIFT = 6
GRID_W = 64
ROPE_BASE = 10000.0
RMS_EPS = 1e-6
GN_EPS = 64e-5
LRU_C = 8.0
N_EXPERTS = 64
TOP_K = 8
EXPERT_FF = 256
ROUTED_SCALE = 2.5
MOE_BLOCK = 256
LORA_W = 32

TM = 256
CH = 64
SCAN_B = 4
HALO = 8
COMB_T = 128
VMEM_LIMIT_BYTES = 48 * 1024 * 1024
EXPERT_VMEM_LIMIT_BYTES = 56 * 1024 * 1024

NN = (((1,), (0,)), ((), ()))
NT = (((1,), (1,)), ((), ()))
TN = (((0,), (0,)), ((), ()))
BNN = (((2,), (1,)), ((0,), (0,)))
BNT = (((2,), (2,)), ((0,), (0,)))
BTN = (((1,), (1,)), ((0,), (0,)))


def _cparams(*sem):
    return pltpu.CompilerParams(dimension_semantics=sem, vmem_limit_bytes=VMEM_LIMIT_BYTES)


def _mm(a, b, dims=NN):
    return lax.dot_general(a, b, dims, preferred_element_type=F32)


def _mmb(a, b, dims=NN):
    return _mm(a.astype(BF16), b.astype(BF16), dims)


def _split2(a):
    hi = a.astype(BF16)
    return hi, (a - hi.astype(F32)).astype(BF16)


def _split3(a):
    hi = a.astype(BF16)
    r = a - hi.astype(F32)
    mid = r.astype(BF16)
    return hi, mid, (r - mid.astype(F32)).astype(BF16)


def _mm3(a, b, dims=NN):
    ah, al = _split2(a)
    bh, bl = _split2(b)
    return _mm(ah, bh, dims) + (_mm(ah, bl, dims) + _mm(al, bh, dims))


def _mmx(a, b_exact, dims=NN):
    ah, am, al = _split3(a)
    return _mm(ah, b_exact, dims) + (_mm(am, b_exact, dims) + _mm(al, b_exact, dims))


def _xmm(a_exact, b, dims=NN):
    bh, bm, bl = _split3(b)
    return _mm(a_exact, bh, dims) + (_mm(a_exact, bm, dims) + _mm(a_exact, bl, dims))


def _iota(shape, dim):
    return lax.broadcasted_iota(I32, shape, dim)


def _head_mask(rows, cols):
    return (_iota((rows, cols), 0) >> HEAD_SHIFT) == (_iota((rows, cols), 1) >> HEAD_SHIFT)


def _head_mask_bf(rows, cols):
    return jnp.where(_head_mask(rows, cols), 1.0, 0.0).astype(BF16)


def _headsum(x, mbd_bf):
    return _mmx(x, mbd_bf)


def _bd(x, mask_bf):
    return jnp.concatenate([x.astype(BF16)] * N_HEADS, axis=1) * mask_bf


def _tri_inverse(a_all, mask_bf):
    def times(x, y):
        n = x.shape[1]
        xh, xl = _split2(x)
        yh, yl = _split2(y)
        top = _mm(jnp.concatenate([xh, xl], axis=1), _bd(yh, mask_bf), BNN)
        return (top[:, 0:n] + top[:, n:2 * n]) + _mm(xh, _bd(yl, mask_bf), BNN)

    t = _iota(a_all.shape, 1)
    j = _iota(a_all.shape, 2) & (CH - 1)
    p = jnp.where(t == j, 1.0, 0.0) + a_all
    a = times(a_all, a_all)
    levels = CH.bit_length() - 2
    for lvl in range(levels):
        if lvl + 1 < levels:
            both = times(jnp.concatenate([a, p], axis=1), a)
            a = both[:, 0:CH]
            p = p + both[:, CH:2 * CH]
        else:
            p = p + times(p, a)
    return p


def _scan_dist(ncols):
    shape = (2 * SCAN_B, CH, ncols)
    t = _iota(shape, 1)
    j = _iota(shape, 2) & (CH - 1)
    return jnp.where(_iota(shape, 0) >= SCAN_B, j - t, t - j)


def _tri_ones():
    return jnp.where(_scan_dist(CH) >= 0, 1.0, 0.0).astype(BF16)


def _chains(fwd_ref, rev_ref):
    return jnp.concatenate([fwd_ref[...], rev_ref[...]], axis=0)


def _rms_modulate(x, g, shift, scale):
    y = x * lax.rsqrt(jnp.mean(x * x, axis=-1, keepdims=True) + RMS_EPS) * g
    return y * (1.0 + scale) + shift


def _softplus(x):
    return jnp.maximum(x, 0.0) + jnp.log(1.0 + jnp.exp(-jnp.abs(x)))


def _silu(x):
    return x * jax.nn.sigmoid(x)


def _gelu_tanh(x):
    return 0.5 * x * (1.0 + jnp.tanh(np.sqrt(2.0 / np.pi).astype(np.float32) * (x + 0.044715 * (x * x * x))))


def _shift_rows(p, prev_row, next_row):
    n = p.shape[0]
    r = _iota((n, 1), 0)
    up = jnp.where(r == 0, prev_row, pltpu.roll(p, 1, 0))
    dn = jnp.where(r == n - 1, next_row, pltpu.roll(p, n - 1, 0))
    return up, dn


def _seq_edges(i, n_ctx_tiles, n_tiles):
    first = jnp.logical_or(i == 0, i == n_ctx_tiles)
    last = jnp.logical_or(i == n_ctx_tiles - 1, i == n_tiles - 1)
    return first, last


def _fill_conv_scratch(scr, cur, prev8, next8, first, last):
    n = cur.shape[0]
    scr[0:HALO, :] = jnp.where(first, 0.0, prev8)
    scr[HALO:HALO + n, :] = cur
    scr[HALO + n:2 * HALO + n, :] = jnp.where(last, 0.0, next8)


def _conv4(scr, w, n):
    return (scr[HALO - 2:HALO - 2 + n, :] * w[0:1, :] + scr[HALO - 1:HALO - 1 + n, :] * w[1:2, :]
            + scr[HALO:HALO + n, :] * w[2:3, :] + scr[HALO + 1:HALO + 1 + n, :] * w[3:4, :])


def _ada_kernel(c_ref, w_ref, b_ref, o_ref):
    c = c_ref[...]
    o_ref[...] = _mm3(_silu(c), w_ref[...]) + b_ref[...]


def _ada_call(cvec, w, b):
    n = w.shape[1]
    tn = 768
    return pl.pallas_call(
        _ada_kernel,
        grid=(n // tn,),
        in_specs=[pl.BlockSpec((8, D_MODEL), lambda j: (0, 0)),
                  pl.BlockSpec((D_MODEL, tn), lambda j: (0, j)),
                  pl.BlockSpec((1, tn), lambda j: (0, j))],
        out_specs=pl.BlockSpec((8, tn), lambda j: (0, j)),
        out_shape=jax.ShapeDtypeStruct((8, n), F32),
        compiler_params=_cparams("arbitrary"),
        name="ada_mod",
    )(cvec, w, b)


IN_COLS = (1024, 1024, 1024, 512, 128)


def _inproj_kernel(x_ref, g_ref, mod_ref, w_ref, rw_ref, ret_ref, gdn_ref, lru_ref, ab_ref):
    h = _rms_modulate(x_ref[...], g_ref[...], mod_ref[0:1, :], mod_ref[1:2, :])
    p = _mm(h.astype(BF16), w_ref[...])
    o = 0
    for ref, n in zip((rw_ref, ret_ref, gdn_ref, lru_ref, ab_ref), IN_COLS):
        ref[...] = p[:, o:o + n]
        o += n


def _mod_spec(nc):
    return pl.BlockSpec((None, None, 6, D_MODEL), lambda b, i: (b, jnp.minimum(i // nc, 1), 0, 0))


def _inproj_call(xs, gain, mod, w, nc):
    B, S, _ = xs.shape
    n = w.shape[1]
    row = lambda width: pl.BlockSpec((None, TM, width), lambda b, i: (b, i, 0))
    return pl.pallas_call(
        _inproj_kernel,
        grid=(B, S // TM),
        in_specs=[row(D_MODEL), pl.BlockSpec((1, D_MODEL), lambda b, i: (0, 0)), _mod_spec(nc),
                  pl.BlockSpec((D_MODEL, n), lambda b, i: (0, 0))],
        out_specs=[row(c) for c in IN_COLS],
        out_shape=[jax.ShapeDtypeStruct((B, S, c), F32) for c in IN_COLS],
        compiler_params=_cparams("arbitrary", "arbitrary"),
        name="in_proj",
    )(xs, gain, mod, w)


def _rwkv_prep_kernel(nc, nt, has_vres, *refs):
    if has_vres:
        (p_ref, pp_ref, pn_ref, vf_ref, mu_ref, w0_ref, w2_ref, a0_ref, a2_ref, g2_ref, kkw_ref, ka_ref, rk_ref,
         v0_ref, v2_ref, sh_ref, dp_ref, bonus_ref, gate_ref) = refs
    else:
        (p_ref, pp_ref, pn_ref, mu_ref, w0_ref, w2_ref, a0_ref, a2_ref, g2_ref, kkw_ref, ka_ref, rk_ref,
         sh_ref, dp_ref, bonus_ref, gate_ref) = refs
    i = pl.program_id(1)
    first, last = _seq_edges(i, nc, nt)
    p = p_ref[...]
    prev_row = jnp.where(first, 0.0, pp_ref[HALO - 1:HALO, :])
    next_row = jnp.where(last, 0.0, pn_ref[0:1, :])
    up, dn = _shift_rows(p, prev_row, next_row)
    ps = p + (0.5 * (up + dn) - p) * mu_ref[...]
    mbd = _head_mask(GROUP_W, GROUP_W).astype(BF16)

    r = ps[:, 0:256]
    k = ps[:, 256:512]
    v = ps[:, 512:768]
    x1 = ps[:, 768:896]
    x2 = ps[:, 896:1024]
    z = w0_ref[...] + _mmb(jnp.tanh(x1), w2_ref[...])
    lw = -np.exp(-0.5).astype(np.float32) * jax.nn.sigmoid(z)
    a = jax.nn.sigmoid(a0_ref[...] + _mmb(x1, a2_ref[...]))
    gate_ref[...] = _mmb(jax.nn.sigmoid(x2), g2_ref[...])
    kk = k * kkw_ref[...]
    kk = kk * lax.rsqrt(_headsum(kk * kk, mbd) + 1e-6)
    if has_vres:
        v = v + (vf_ref[...] - v) * jax.nn.sigmoid(v0_ref[...] + _mmb(x2, v2_ref[...]))
    sh_ref[:, 0:256] = r
    sh_ref[:, 256:512] = kk
    sh_ref[:, 512:768] = v
    ksum = jnp.zeros_like(k)
    for d in range(2):
        a_d = a[:, d * 256:(d + 1) * 256]
        kd = k * (1.0 + (a_d - 1.0) * ka_ref[...])
        ksum = ksum + kd
        dp_ref[d, :, 0:256] = lw[:, d * 256:(d + 1) * 256]
        dp_ref[d, :, 256:512] = kd
        dp_ref[d, :, 512:768] = kk * a_d
    bonus_ref[...] = _headsum(r * ksum * rk_ref[...], mbd) * v


def _halo_specs(width, nt):
    per = TM // HALO
    cur = pl.BlockSpec((None, TM, width), lambda b, i: (b, i, 0))
    prev = pl.BlockSpec((None, HALO, width), lambda b, i: (b, jnp.maximum(i * per - 1, 0), 0))
    nxt = pl.BlockSpec((None, HALO, width), lambda b, i: (b, jnp.minimum((i + 1) * per, nt * per - 1), 0))
    return cur, prev, nxt


def _const_spec(a):
    nd = a.ndim
    return pl.BlockSpec(a.shape, lambda b, i: (0,) * nd)


def _rwkv_prep_call(p_rw, vfirst_pack, prm, nc):
    B, S, _ = p_rw.shape
    nt = S // TM
    has_vres = vfirst_pack is not None
    cur, prev, nxt = _halo_specs(1024, nt)
    ins = [p_rw, p_rw, p_rw]
    specs = [cur, prev, nxt]
    if has_vres:
        ins.append(vfirst_pack)
        specs.append(pl.BlockSpec((None, TM, 256), lambda b, i: (b, i, 2)))
    names = ["mu", "w0", "w2", "a0", "a2", "g2", "kk", "ka", "rk"] + (["v0", "v2"] if has_vres else [])
    for nme in names:
        ins.append(prm[nme])
        specs.append(_const_spec(prm[nme]))
    row = lambda w: pl.BlockSpec((None, TM, w), lambda b, i: (b, i, 0))
    return pl.pallas_call(
        functools.partial(_rwkv_prep_kernel, nc, nt, has_vres),
        grid=(B, nt),
        in_specs=specs,
        out_specs=[row(768), pl.BlockSpec((None, 2, TM, 768), lambda b, i: (b, 0, i, 0)), row(256), row(256)],
        out_shape=[jax.ShapeDtypeStruct((B, S, 768), F32), jax.ShapeDtypeStruct((B, 2, S, 768), F32),
                   jax.ShapeDtypeStruct((B, S, 256), F32), jax.ShapeDtypeStruct((B, S, 256), F32)],
        compiler_params=_cparams("arbitrary", "arbitrary"),
        name="rwkv_prep",
    )(*ins)


def _rwkv_chains(sh, dp, s_ref, mask, mask_bf):
    r, kk, v = sh[:, :, 0:256], sh[:, :, 256:512], sh[:, :, 512:768]
    lw, kd, bb = dp[:, :, 0:256], dp[:, :, 256:512], dp[:, :, 512:768]
    g = _xmm(_tri_ones(), lw, BNN)
    g_tot = jnp.sum(lw, axis=1, keepdims=True)
    e_tot = jnp.exp(g_tot)
    eng = jnp.exp(-g)
    ab = -kk * jnp.exp(g - lw)
    bbar = bb * eng
    kbar = kd * eng
    rbar = r * jnp.exp(g)
    dist = _scan_dist(N_HEADS * CH)
    strict, incl = dist > 0, dist >= 0
    abb, rbb, vb = ab.astype(BF16), rbar.astype(BF16), v.astype(BF16)
    sc = _mm(jnp.concatenate([abb, rbb], axis=1),
             jnp.concatenate([_bd(bbar, mask_bf), _bd(kbar, mask_bf)], axis=1), BNT)
    a_ab = jnp.where(strict, sc[:, 0:CH, 0:256], 0.0)
    a_ak = jnp.where(strict, sc[:, 0:CH, 256:512], 0.0)
    a_rb = jnp.where(incl, sc[:, CH:2 * CH, 0:256], 0.0)
    a_rk = jnp.where(incl, sc[:, CH:2 * CH, 256:512], 0.0)
    tb = _tri_inverse(a_ab, mask_bf).astype(BF16)
    vbd = _bd(vb, mask_bf)
    av = _mm(jnp.concatenate([a_ak, a_rk], axis=1).astype(BF16), vbd, BNN)
    rhs, y0 = av[:, 0:CH], av[:, CH:2 * CH]
    wt = _mm(tb, _bd(abb, mask_bf), BNN)
    u0 = _mm(tb, _bd(rhs, mask_bf), BNN)
    s = s_ref[...]
    ws = _mm(jnp.concatenate([wt.astype(BF16), rbb], axis=1), s.astype(BF16), BNT)
    u = ws[:, 0:CH] + u0
    ub = u.astype(BF16)
    y = ws[:, CH:2 * CH] + _mm(a_rb.astype(BF16), _bd(ub, mask_bf), BNN) + y0
    upd = _mm(jnp.concatenate([ub, vb], axis=1),
              jnp.concatenate([(bbar * e_tot).astype(BF16), (kbar * e_tot).astype(BF16)], axis=1), BTN)
    s_ref[...] = s * e_tot + jnp.where(mask, upd, 0.0)
    return y


def _rwkv_scan_kernel(sh0_ref, sh1_ref, dp0_ref, dp1_ref, y0_ref, y1_ref, s_ref):
    @pl.when(pl.program_id(1) == 0)
    def _():
        s_ref[...] = jnp.zeros_like(s_ref)

    y = _rwkv_chains(_chains(sh0_ref, sh1_ref), _chains(dp0_ref, dp1_ref), s_ref,
                     _head_mask(GROUP_W, GROUP_W), _head_mask_bf(GROUP_W, GROUP_W))
    y0_ref[...] = y[0:SCAN_B]
    y1_ref[...] = y[SCAN_B:2 * SCAN_B]


def _rev_idx(n, n_ctx, n_all):
    return jnp.where(n < n_ctx, n_ctx - 1 - n, n_all + n_ctx - 1 - n)


def _scan_specs(width_sh, width_dp, ncc, nch):
    fwd = lambda b, n: (b, n, 0)
    rev = lambda b, n: (b, _rev_idx(n, ncc, nch), 0)
    fwd_d = lambda b, n: (b, 0, n, 0)
    rev_d = lambda b, n: (b, 1, _rev_idx(n, ncc, nch), 0)
    return ([pl.BlockSpec((SCAN_B, CH, width_sh), fwd), pl.BlockSpec((SCAN_B, CH, width_sh), rev),
             pl.BlockSpec((SCAN_B, None, CH, width_dp), fwd_d), pl.BlockSpec((SCAN_B, None, CH, width_dp), rev_d)],
            [pl.BlockSpec((SCAN_B, CH, GROUP_W), fwd), pl.BlockSpec((SCAN_B, CH, GROUP_W), rev)])


def _state_scratch():
    return [pltpu.VMEM((2 * SCAN_B, GROUP_W, GROUP_W), F32)]


def _rwkv_scan_call(sh, dp, ncc):
    B, S, _ = sh.shape
    nch = S // CH
    in_specs, out_specs = _scan_specs(768, 768, ncc, nch)
    return pl.pallas_call(
        _rwkv_scan_kernel,
        grid=(B // SCAN_B, nch),
        in_specs=in_specs,
        out_specs=out_specs,
        out_shape=[jax.ShapeDtypeStruct((B, S, GROUP_W), F32)] * 2,
        scratch_shapes=_state_scratch(),
        compiler_params=_cparams("arbitrary", "arbitrary"),
        name="rwkv_scan",
    )(sh, sh, dp, dp)


def _per_dir(fwd, rev):
    f = jnp.broadcast_to(fwd[None], (SCAN_B,) + fwd.shape)
    r = jnp.broadcast_to(rev[None], (SCAN_B,) + rev.shape)
    return jnp.concatenate([f, r], axis=0)


def _ret_chains(p, cs, lg, r_ref, mask, mask_bf):
    nc = 2 * SCAN_B
    p2 = p.reshape(nc * CH, 1024)
    cs2 = cs.reshape(nc * CH, 512)
    lane = _iota((nc * CH, GROUP_W), 1)
    first_half = (lane & (HEAD_DIM - 1)) < HEAD_DIM // 2
    cos, sin = cs2[:, 0:256], cs2[:, 256:512]

    def rope(t):
        swapped = jnp.where(first_half, pltpu.roll(t, GROUP_W - HEAD_DIM // 2, 1), pltpu.roll(t, HEAD_DIM // 2, 1))
        return (t * cos + swapped * sin).reshape(nc, CH, GROUP_W)

    q = rope(p2[:, 0:256])
    k = rope(p2[:, 256:512]) * (HEAD_DIM ** -0.5)
    v = p[:, :, 512:768]
    dist = _scan_dist(N_HEADS * CH)
    decay = jnp.where(dist >= 0, jnp.exp(jnp.maximum(dist, 0).astype(F32) * lg), 0.0)
    shape = (nc, CH, GROUP_W)
    tpos = _iota(shape, 1)
    done = jnp.where(_iota(shape, 0) >= SCAN_B, CH - 1 - tpos, tpos)
    xi = jnp.exp((done + 1).astype(F32) * lg)
    zeta = jnp.exp((CH - 1 - done).astype(F32) * lg)
    g_c = jnp.exp(float(CH) * lg)
    vb = v.astype(BF16)
    s = _mm(q.astype(BF16), _bd(k, mask_bf), BNT) * decay
    rr = r_ref[...]
    o = _mm(s.astype(BF16), _bd(vb, mask_bf), BNN) + _mm((q * xi).astype(BF16), rr.astype(BF16), BNN)
    r_ref[...] = rr * g_c + jnp.where(mask, _mm((k * zeta).astype(BF16), vb, BTN), 0.0)
    return o


def _ret_kernel(p0_ref, p1_ref, cs0_ref, cs1_ref, lg_ref, o0_ref, o1_ref, r_ref):
    @pl.when(pl.program_id(1) == 0)
    def _():
        r_ref[...] = jnp.zeros_like(r_ref)

    o = _ret_chains(_chains(p0_ref, p1_ref), _per_dir(cs0_ref[...], cs1_ref[...]),
                    _per_dir(lg_ref[0:1, :], lg_ref[1:2, :]), r_ref,
                    _head_mask(GROUP_W, GROUP_W), _head_mask_bf(GROUP_W, GROUP_W))
    o0_ref[...] = o[0:SCAN_B]
    o1_ref[...] = o[SCAN_B:2 * SCAN_B]


def _ret_call(p_ret, cs, lg, ncc):
    B, S, _ = p_ret.shape
    nch = S // CH
    fwd = lambda b, n: (b, n, 0)
    rev = lambda b, n: (b, _rev_idx(n, ncc, nch), 0)
    fwd2 = lambda b, n: (n, 0)
    rev2 = lambda b, n: (_rev_idx(n, ncc, nch), 0)
    return pl.pallas_call(
        _ret_kernel,
        grid=(B // SCAN_B, nch),
        in_specs=[pl.BlockSpec((SCAN_B, CH, 1024), fwd), pl.BlockSpec((SCAN_B, CH, 1024), rev),
                  pl.BlockSpec((CH, 512), fwd2), pl.BlockSpec((CH, 512), rev2),
                  pl.BlockSpec((2, GROUP_W), lambda b, n: (0, 0))],
        out_specs=[pl.BlockSpec((SCAN_B, CH, GROUP_W), fwd), pl.BlockSpec((SCAN_B, CH, GROUP_W), rev)],
        out_shape=[jax.ShapeDtypeStruct((B, S, GROUP_W), F32)] * 2,
        scratch_shapes=_state_scratch(),
        compiler_params=_cparams("arbitrary", "arbitrary"),
        name="retention",
    )(p_ret, p_ret, cs, cs, lg)


def _gdn_prep_kernel(nc, nt, p_ref, pp_ref, pn_ref, ab_ref, cw_ref, alog_ref, dtb_ref, e_ref, sh_ref, dp_ref, scr):
    i = pl.program_id(1)
    first, last = _seq_edges(i, nc, nt)
    _fill_conv_scratch(scr, p_ref[:, 0:768], pp_ref[:, 0:768], pn_ref[:, 0:768], first, last)
    qkv = _silu(_conv4(scr, cw_ref[...], TM))
    mbd = _head_mask(GROUP_W, GROUP_W).astype(BF16)
    q, k = qkv[:, 0:256], qkv[:, 256:512]
    sh_ref[:, 0:256] = q * lax.rsqrt(_headsum(q * q, mbd) + 1e-6) * (HEAD_DIM ** -0.5)
    sh_ref[:, 256:512] = k * lax.rsqrt(_headsum(k * k, mbd) + 1e-6)
    sh_ref[:, 512:768] = qkv[:, 512:768]
    ab = ab_ref[...]
    ld = -jnp.exp(alog_ref[...]) * _softplus(ab + dtb_ref[...])
    vec = jnp.where(_iota(ab.shape, 1) < 2 * N_HEADS, ld, jax.nn.sigmoid(ab))
    ex = _mmx(vec, e_ref[...])
    for d in range(2):
        dp_ref[d, :, 0:256] = ex[:, d * 256:(d + 1) * 256]
        dp_ref[d, :, 256:512] = ex[:, 512 + d * 256:512 + (d + 1) * 256]


def _gdn_prep_call(p_gdn, p_ab, prm, nc):
    B, S, _ = p_gdn.shape
    nt = S // TM
    cur, prev, nxt = _halo_specs(1024, nt)
    consts = [prm["conv_w"], prm["alog"], prm["dtb"], prm["expand"]]
    row = lambda w: pl.BlockSpec((None, TM, w), lambda b, i: (b, i, 0))
    return pl.pallas_call(
        functools.partial(_gdn_prep_kernel, nc, nt),
        grid=(B, nt),
        in_specs=[cur, prev, nxt, row(128)] + [_const_spec(a) for a in consts],
        out_specs=[row(768), pl.BlockSpec((None, 2, TM, 512), lambda b, i: (b, 0, i, 0))],
        out_shape=[jax.ShapeDtypeStruct((B, S, 768), F32), jax.ShapeDtypeStruct((B, 2, S, 512), F32)],
        scratch_shapes=[pltpu.VMEM((TM + 2 * HALO, 768), F32)],
        compiler_params=_cparams("arbitrary", "arbitrary"),
        name="gdn_prep",
    )(p_gdn, p_gdn, p_gdn, p_ab, *consts)


def _gdn_chains(sh, dp, s_ref, mask, mask_bf):
    q, k, v = sh[:, :, 0:256], sh[:, :, 256:512], sh[:, :, 512:768]
    ld, beta = dp[:, :, 0:256], dp[:, :, 256:512]
    gc = _xmm(_tri_ones(), ld, BNN)
    g_tot = jnp.sum(ld, axis=1, keepdims=True)
    dist = _scan_dist(N_HEADS * CH)
    gc_cols = jnp.sum(jnp.where(dist == 0, gc, 0.0), axis=1, keepdims=True)
    gam = jnp.where(dist >= 0, jnp.exp(jnp.minimum(gc - gc_cols, 0.0)), 0.0)
    kb = k * beta
    kk = _mm(jnp.concatenate([kb, q], axis=1).astype(BF16), _bd(k, mask_bf), BNT)
    a_mat = jnp.where(dist > 0, kk[:, 0:CH] * gam, 0.0)
    attn = kk[:, CH:2 * CH] * gam
    tb = _tri_inverse(-a_mat, mask_bf).astype(BF16)
    egc = jnp.exp(gc)
    u = _mm(tb, _bd(v * beta, mask_bf), BNN)
    w = _mm(tb, _bd(kb * egc, mask_bf), BNN)
    s = s_ref[...]
    ws = _mm(jnp.concatenate([w, q * egc], axis=1).astype(BF16), s.astype(BF16), BNN)
    v_new = u - ws[:, 0:CH]
    vnb = v_new.astype(BF16)
    o = ws[:, CH:2 * CH] + _mm(attn.astype(BF16), _bd(vnb, mask_bf), BNN)
    upd = _mm((k * jnp.exp(g_tot - gc)).astype(BF16), vnb, BTN)
    s_ref[...] = s * jnp.exp(g_tot) + jnp.where(mask, upd, 0.0)
    return o


def _gdn_scan_kernel(sh0_ref, sh1_ref, dp0_ref, dp1_ref, o0_ref, o1_ref, s_ref):
    @pl.when(pl.program_id(1) == 0)
    def _():
        s_ref[...] = jnp.zeros_like(s_ref)

    o = _gdn_chains(_chains(sh0_ref, sh1_ref), _chains(dp0_ref, dp1_ref), s_ref,
                    _head_mask(GROUP_W, GROUP_W), _head_mask_bf(GROUP_W, GROUP_W))
    o0_ref[...] = o[0:SCAN_B]
    o1_ref[...] = o[SCAN_B:2 * SCAN_B]


def _gdn_scan_call(sh, dp, ncc):
    B, S, _ = sh.shape
    nch = S // CH
    in_specs, out_specs = _scan_specs(768, 512, ncc, nch)
    return pl.pallas_call(
        _gdn_scan_kernel,
        grid=(B // SCAN_B, nch),
        in_specs=in_specs,
        out_specs=out_specs,
        out_shape=[jax.ShapeDtypeStruct((B, S, GROUP_W), F32)] * 2,
        scratch_shapes=_state_scratch(),
        compiler_params=_cparams("arbitrary", "arbitrary"),
        name="gdn_scan",
    )(sh, sh, dp, dp)


def _lru_tile(d, rev, first, last, p_ref, pp_ref, pn_ref, cw_ref, cb_ref, wa_ref, ba_ref, wx_ref, bx_ref, lam_ref,
              h_ref, carry_ref, scr):
    _fill_conv_scratch(scr, p_ref[:, 0:256], pp_ref[:, 0:256], pn_ref[:, 0:256], first, last)
    xm = _conv4(scr, cw_ref[...], TM) + cb_ref[...]
    xb = xm.astype(BF16)
    r = jax.nn.sigmoid(_mm(xb, wa_ref[d]) + ba_ref[d:d + 1, :])
    ig = jax.nn.sigmoid(_mm(xb, wx_ref[d]) + bx_ref[d:d + 1, :])
    log_a = -LRU_C * _softplus(-lam_ref[d:d + 1, :]) * r
    a = jnp.exp(log_a)
    th = jnp.tanh(log_a)
    b = jnp.sqrt(-2.0 * th / (1.0 - th)) * ig * xm
    row = _iota((TM, 1), 0)
    s = 1
    while s < TM:
        if rev:
            ok = row < TM - s
            a_n, b_n = pltpu.roll(a, TM - s, 0), pltpu.roll(b, TM - s, 0)
        else:
            ok = row >= s
            a_n, b_n = pltpu.roll(a, s, 0), pltpu.roll(b, s, 0)
        b = jnp.where(ok, b + a * b_n, b)
        a = jnp.where(ok, a * a_n, a)
        s *= 2
    h = b + a * carry_ref[0:1, :]
    h_ref[...] = h
    end = h[0:1, :] if rev else h[TM - 1:TM, :]
    carry_ref[...] = jnp.broadcast_to(end, carry_ref.shape)


def _lru_kernel(nc, nt, p0, pp0, pn0, p1, pp1, pn1, cw, cb, wa, ba, wx, bx, lam, h0_ref, h1_ref, c0, c1, scr):
    i = pl.program_id(1)

    @pl.when(i == 0)
    def _():
        c0[...] = jnp.zeros_like(c0)
        c1[...] = jnp.zeros_like(c1)

    first, last = _seq_edges(i, nc, nt)
    _lru_tile(0, False, first, last, p0, pp0, pn0, cw, cb, wa, ba, wx, bx, lam, h0_ref, c0, scr)
    j = _rev_idx(i, nc, nt)
    first, last = _seq_edges(j, nc, nt)
    _lru_tile(1, True, first, last, p1, pp1, pn1, cw, cb, wa, ba, wx, bx, lam, h1_ref, c1, scr)


def _lru_call(p_lru, prm, nc):
    B, S, _ = p_lru.shape
    nt = S // TM
    per = TM // HALO
    rj = lambda i: _rev_idx(i, nc, nt)
    cur, prev, nxt = _halo_specs(512, nt)
    cur_r = pl.BlockSpec((None, TM, 512), lambda b, i: (b, rj(i), 0))
    prev_r = pl.BlockSpec((None, HALO, 512), lambda b, i: (b, jnp.maximum(rj(i) * per - 1, 0), 0))
    nxt_r = pl.BlockSpec((None, HALO, 512), lambda b, i: (b, jnp.minimum((rj(i) + 1) * per, nt * per - 1), 0))
    consts = [prm[k] for k in ("conv_w", "conv_b", "wa", "ba", "wx", "bx", "lam")]
    return pl.pallas_call(
        functools.partial(_lru_kernel, nc, nt),
        grid=(B, nt),
        in_specs=[cur, prev, nxt, cur_r, prev_r, nxt_r] + [_const_spec(a) for a in consts],
        out_specs=[pl.BlockSpec((None, TM, GROUP_W), lambda b, i: (b, i, 0)),
                   pl.BlockSpec((None, TM, GROUP_W), lambda b, i: (b, rj(i), 0))],
        out_shape=[jax.ShapeDtypeStruct((B, S, GROUP_W), F32)] * 2,
        scratch_shapes=[pltpu.VMEM((8, GROUP_W), F32), pltpu.VMEM((8, GROUP_W), F32),
                        pltpu.VMEM((TM + 2 * HALO, GROUP_W), F32)],
        compiler_params=_cparams("arbitrary", "arbitrary"),
        name="rglru",
    )(p_lru, p_lru, p_lru, p_lru, p_lru, p_lru, *consts)


def _mix_out_kernel(y0, y1, bonus, grw, ro0, ro1, gret, go0, go1, ggdn, lh0, lh1, glru, x_ref, mod_ref, w_ref,
                    lnw, lnb, gnw, o_ref):
    mbd = _head_mask(GROUP_W, GROUP_W).astype(BF16)
    inv = 1.0 / HEAD_DIM

    def head_rms(o):
        return o * lax.rsqrt(_headsum(o * o, mbd) * inv + RMS_EPS)

    y = y0[...] + y1[...]
    mu = _headsum(y, mbd) * inv
    dy = y - mu
    var = _headsum(dy * dy, mbd) * inv
    a = (dy * lax.rsqrt(var + GN_EPS) * lnw[...] + lnb[...] + bonus[...]) * grw[...]
    b = head_rms(ro0[...] + ro1[...]) * _silu(gret[...])
    g = head_rms(go0[...] + go1[...]) * gnw[...] * _silu(ggdn[...])
    r = (lh0[...] + lh1[...]) * _gelu_tanh(glru[...])
    w = w_ref[...]
    mix = (_mm(a.astype(BF16), w[0:256]) + _mm(b.astype(BF16), w[256:512])
           + _mm(g.astype(BF16), w[512:768]) + _mm(r.astype(BF16), w[768:1024]))
    o_ref[...] = x_ref[...] + mod_ref[2:3, :] * mix


def _mix_out_call(ys, bonus, grw, ros, p_ret, gos, p_gdn, lhs, p_lru, xs, mod, w_out, lnw, lnb, gnw, nc):
    B, S, _ = xs.shape
    row = lambda w, c=0: pl.BlockSpec((None, TM, w), lambda b, i: (b, i, c))
    g = row(GROUP_W)
    consts = [lnw, lnb, gnw]
    return pl.pallas_call(
        _mix_out_kernel,
        grid=(B, S // TM),
        in_specs=[g, g, g, g, g, g, row(GROUP_W, 3), g, g, row(GROUP_W, 3), g, g, row(GROUP_W, 1),
                  row(D_MODEL), _mod_spec(nc), pl.BlockSpec((D_MODEL, D_MODEL), lambda b, i: (0, 0))]
                 + [_const_spec(a) for a in consts],
        out_specs=row(D_MODEL),
        out_shape=jax.ShapeDtypeStruct((B, S, D_MODEL), F32),
        compiler_params=_cparams("arbitrary", "arbitrary"),
        name="mix_out",
    )(ys[0], ys[1], bonus, grw, ros[0], ros[1], p_ret, gos[0], gos[1], p_gdn, lhs[0], lhs[1], p_lru,
      xs, mod, w_out, *consts)


def _moe_pre_kernel(x_ref, g_ref, mod_ref, rw_ref, rb_ref, sg_ref, su_ref, sd_ref, h_ref, idx_ref, gate_ref, sh_ref):
    h = _rms_modulate(x_ref[...], g_ref[...], mod_ref[3:4, :], mod_ref[4:5, :])
    hb = h.astype(BF16)
    bits = pltpu.bitcast(hb.astype(F32), jnp.uint32)
    half = D_MODEL // 2
    h_ref[...] = (bits[:, 0:half] >> 16) | (bits[:, half:D_MODEL] & jnp.uint32(0xFFFF0000))
    scores = jax.nn.sigmoid(_mm3(h, rw_ref[...]))
    sel = scores + rb_ref[...]
    lane = _iota(scores.shape, 1)
    lane_f = lane.astype(F32)
    gates = jnp.zeros_like(scores)
    idxs = jnp.zeros_like(scores)
    total = jnp.zeros((scores.shape[0], 1), F32)
    for kk in range(TOP_K):
        m = jnp.max(sel, axis=-1, keepdims=True)
        idx = jnp.min(jnp.where(sel == m, lane_f, 2.0 * N_EXPERTS), axis=-1, keepdims=True)
        hit = lane_f == idx
        gk = jnp.sum(jnp.where(hit, scores, 0.0), axis=-1, keepdims=True)
        total = total + gk
        gates = jnp.where(lane == kk, gk, gates)
        idxs = jnp.where(lane == kk, idx, idxs)
        sel = jnp.where(hit, -jnp.inf, sel)
    gate_ref[...] = ROUTED_SCALE * gates / total
    idx_ref[...] = idxs.astype(I32)
    act = _silu(_mm(hb, sg_ref[...])) * _mm(hb, su_ref[...])
    sh_ref[...] = _mm(act.astype(BF16), sd_ref[...])


def _moe_pre_call(xs, gain, mod, rw, rb, sg, su, sd, nc):
    B, S, _ = xs.shape
    row = lambda w: pl.BlockSpec((None, TM, w), lambda b, i: (b, i, 0))
    consts = [rw, rb, sg, su, sd]
    return pl.pallas_call(
        _moe_pre_kernel,
        grid=(B, S // TM),
        in_specs=[row(D_MODEL), pl.BlockSpec((1, D_MODEL), lambda b, i: (0, 0)), _mod_spec(nc)]
                 + [_const_spec(a) for a in consts],
        out_specs=[row(D_MODEL // 2), row(128), row(128), row(D_MODEL)],
        out_shape=[jax.ShapeDtypeStruct((B, S, D_MODEL // 2), jnp.uint32), jax.ShapeDtypeStruct((B, S, 128), I32),
                   jax.ShapeDtypeStruct((B, S, 128), F32), jax.ShapeDtypeStruct((B, S, D_MODEL), F32)],
        compiler_params=_cparams("arbitrary", "arbitrary"),
        name="moe_pre",
    )(xs, gain, mod, *consts)


def _row_out_copy(src, dst_hbm, sem, r, dst_row):
    return pltpu.make_async_copy(src.at[pl.ds(r, 1), :], dst_hbm.at[pl.ds(dst_row, 1), :], sem)


def _wait_block_out(yb_slot, y_hbm, sem):
    for r in range(MOE_BLOCK):
        _row_out_copy(yb_slot, y_hbm, sem, r, 0).wait()


def _start_block_out(yb_slot, y_hbm, sem, dst_ref):
    for r in range(MOE_BLOCK):
        _row_out_copy(yb_slot, y_hbm, sem, r, dst_ref[0, 0, r]).start()


def _expert_kernel(blk_e_ref, tok_ref, dst_ref, dstp_ref, hp_hbm, wg_ref, wu_ref, wd_ref, y_hbm,
                   tab, xb, yb, tsem, ssem):
    i = pl.program_id(0)
    nb = pl.num_programs(0)
    slot = lax.rem(i, 2)
    other = 1 - slot

    @pl.when(i == 0)
    def _():
        load = pltpu.make_async_copy(hp_hbm, tab, tsem)
        load.start()
        yb[1] = jnp.zeros((MOE_BLOCK, D_MODEL), F32)
        load.wait()

    _start_block_out(yb.at[other], y_hbm, ssem.at[other], dstp_ref)
    for r in range(MOE_BLOCK):
        xb[pl.ds(r, 1), :] = tab[pl.ds(tok_ref[0, 0, r], 1), :]
    u = xb[...]
    lo = pltpu.bitcast(u << 16, F32).astype(BF16)
    hi = pltpu.bitcast(u & jnp.uint32(0xFFFF0000), F32).astype(BF16)
    x = jnp.concatenate([lo, hi], axis=1)
    act = _silu(_mm(x, wg_ref[...])) * _mm(x, wu_ref[...])
    y = _mm(act.astype(BF16), wd_ref[...])

    @pl.when(i >= 1)
    def _():
        _wait_block_out(yb.at[slot], y_hbm, ssem.at[slot])

    yb[slot] = y

    @pl.when(i == nb - 1)
    def _():
        _start_block_out(yb.at[slot], y_hbm, ssem.at[slot], dst_ref)
        _wait_block_out(yb.at[other], y_hbm, ssem.at[other])
        _wait_block_out(yb.at[slot], y_hbm, ssem.at[slot])


def _expert_call(blk_e, slot_tok, slot_dst, dst_prev, hp_flat, wg, wu, wd, n_rows):
    nb = blk_e.shape[0]
    ff = wg.shape[2]
    n_tok, half = hp_flat.shape
    idx_spec = pl.BlockSpec((1, 1, MOE_BLOCK), lambda i, be: (i, 0, 0), memory_space=pltpu.SMEM)
    grid_spec = pltpu.PrefetchScalarGridSpec(
        num_scalar_prefetch=1,
        grid=(nb,),
        in_specs=[idx_spec, idx_spec, idx_spec,
                  pl.BlockSpec(memory_space=pl.ANY),
                  pl.BlockSpec((None, D_MODEL, ff), lambda i, be: (be[i], 0, 0)),
                  pl.BlockSpec((None, D_MODEL, ff), lambda i, be: (be[i], 0, 0)),
                  pl.BlockSpec((None, ff, D_MODEL), lambda i, be: (be[i], 0, 0))],
        out_specs=pl.BlockSpec(memory_space=pl.ANY),
        scratch_shapes=[pltpu.VMEM((n_tok, half), jnp.uint32), pltpu.VMEM((MOE_BLOCK, half), jnp.uint32),
                        pltpu.VMEM((2, MOE_BLOCK, D_MODEL), F32),
                        pltpu.SemaphoreType.DMA(()), pltpu.SemaphoreType.DMA((2,))],
    )
    return pl.pallas_call(
        _expert_kernel,
        grid_spec=grid_spec,
        out_shape=jax.ShapeDtypeStruct((n_rows, D_MODEL), F32),
        compiler_params=pltpu.CompilerParams(dimension_semantics=("arbitrary",),
                                             vmem_limit_bytes=EXPERT_VMEM_LIMIT_BYTES),
        name="moe_experts",
    )(blk_e, slot_tok, slot_dst, dst_prev, hp_flat, wg, wu, wd)


def _combined(y_ref, gate_ref, sh_ref, x_ref, mod_ref):
    gate = gate_ref[...]
    routed = jnp.zeros((COMB_T, D_MODEL), F32)
    for k in range(TOP_K):
        routed = routed + gate[:, k:k + 1] * y_ref[k * COMB_T:(k + 1) * COMB_T, :]
    return x_ref[...] + mod_ref[5:6, :] * (routed + sh_ref[...])


def _combine_kernel(y_ref, gate_ref, sh_ref, x_ref, mod_ref, o_ref):
    o_ref[...] = _combined(y_ref, gate_ref, sh_ref, x_ref, mod_ref)


def _combine_norm_kernel(y_ref, gate_ref, sh_ref, x_ref, mod_ref, g_ref, o_ref):
    x = _combined(y_ref, gate_ref, sh_ref, x_ref, mod_ref)
    o_ref[...] = x * lax.rsqrt(jnp.mean(x * x, axis=-1, keepdims=True) + RMS_EPS) * g_ref[...]


def _combine_call(y, gate, shared, xs, mod, nc, final_gain=None):
    B, S, _ = xs.shape
    per = S // COMB_T
    skip = 0 if final_gain is None else nc * TM // COMB_T
    row = lambda w: pl.BlockSpec((None, COMB_T, w), lambda b, i: (b, i + skip, 0))
    mod_spec = pl.BlockSpec((None, None, 6, D_MODEL),
                            lambda b, i: (b, jnp.minimum((i + skip) // (nc * TM // COMB_T), 1), 0, 0))
    in_specs = [pl.BlockSpec((COMB_T * TOP_K, D_MODEL), lambda b, i: (b * per + i + skip, 0)),
                row(128), row(D_MODEL), row(D_MODEL), mod_spec]
    args = [y, gate, shared, xs, mod]
    if final_gain is not None:
        in_specs.append(pl.BlockSpec((1, D_MODEL), lambda b, i: (0, 0)))
        args.append(final_gain)
    return pl.pallas_call(
        _combine_kernel if final_gain is None else _combine_norm_kernel,
        grid=(B, per - skip),
        in_specs=in_specs,
        out_specs=pl.BlockSpec((None, COMB_T, D_MODEL), lambda b, i: (b, i, 0)),
        out_shape=jax.ShapeDtypeStruct((B, S - skip * COMB_T, D_MODEL), F32),
        compiler_params=_cparams("arbitrary", "arbitrary"),
        name="moe_combine",
    )(*args)


N_RWKV, N_RET, N_GDN, N_LRU = 960, 1024, 1040, 512


def _pack_w_in(w_in_l, w_vres_l):
    o_ret = N_RWKV
    o_gdn = o_ret + N_RET
    o_lru = o_gdn + N_GDN
    z = lambda n: jnp.zeros((D_MODEL, n), F32)
    vres = z(LORA_W) if w_vres_l is None else w_vres_l
    cols = [w_in_l[:, 0:N_RWKV], vres, z(1024 - N_RWKV - LORA_W),
            w_in_l[:, o_ret:o_gdn],
            w_in_l[:, o_gdn:o_gdn + 768], w_in_l[:, o_gdn + 784:o_lru],
            w_in_l[:, o_lru:o_lru + N_LRU],
            w_in_l[:, o_gdn + 768:o_gdn + 784], z(112)]
    return jnp.concatenate(cols, axis=1).astype(BF16)


def _block_diag2(w):
    r, c = w.shape[1], w.shape[2]
    z = jnp.zeros((r, c), F32)
    return jnp.concatenate([jnp.concatenate([w[0], z], 1), jnp.concatenate([z, w[1]], 1)], 0)


def _rwkv_params(l, rw_mu, rw_w0, rw_w2, rw_a0, rw_a2, rw_g2, rw_kk, rw_ka, rw_rk, rw_v0, rw_v2):
    z64 = jnp.zeros((2 * LORA_W, 2 * GROUP_W), F32)
    prm = {
        "mu": jnp.pad(rw_mu[l], (0, 1024 - N_RWKV))[None, :],
        "w0": rw_w0[l].reshape(1, 2 * GROUP_W),
        "w2": jnp.concatenate([_block_diag2(rw_w2[l]), z64], 0),
        "a0": rw_a0[l].reshape(1, 2 * GROUP_W),
        "a2": jnp.concatenate([z64, _block_diag2(rw_a2[l])], 0),
        "g2": jnp.concatenate([rw_g2[l], jnp.zeros((64, GROUP_W), F32)], 0),
        "kk": rw_kk[l][None, :],
        "ka": rw_ka[l][None, :],
        "rk": rw_rk[l].reshape(1, GROUP_W),
    }
    if l > 0:
        prm["v0"] = rw_v0[l - 1][None, :]
        prm["v2"] = jnp.concatenate([jnp.zeros((64, GROUP_W), F32), rw_v2[l - 1],
                                     jnp.zeros((128 - 64 - LORA_W, GROUP_W), F32)], 0)
    return prm


def _head_expand():
    e = np.zeros((128, 1024), np.float32)
    for c in range(4 * N_HEADS):
        grp, h = divmod(c, N_HEADS)
        e[c, grp * GROUP_W + h * HEAD_DIM: grp * GROUP_W + (h + 1) * HEAD_DIM] = 1.0
    return jnp.asarray(e, BF16)


def _lanes_per_head(v):
    return jnp.repeat(v, HEAD_DIM, axis=-1)


def _block_diag_heads(w):
    out = jnp.zeros((GROUP_W, GROUP_W), F32)
    for h in range(N_HEADS):
        out = out.at[h * HEAD_DIM:(h + 1) * HEAD_DIM, h * HEAD_DIM:(h + 1) * HEAD_DIM].set(w[h])
    return out


def _rope_tables(tc, tx):
    t = np.arange(tx)
    rows = (t // GRID_W).astype(np.float32)
    cols = (t % GRID_W).astype(np.float32)
    nf = HEAD_DIM // 4
    inv = (ROPE_BASE ** (-np.arange(nf, dtype=np.float32) / nf)).astype(np.float32)
    ang = jnp.concatenate([jnp.asarray(rows)[:, None] * inv, jnp.asarray(cols)[:, None] * inv], -1)
    cos, sin = jnp.cos(ang), jnp.sin(ang)
    cos_h = jnp.concatenate([cos, cos], -1)
    sin_h = jnp.concatenate([-sin, sin], -1)
    cos_f = jnp.tile(cos_h, (1, N_HEADS))
    sin_f = jnp.tile(sin_h, (1, N_HEADS))
    cos_f = jnp.concatenate([jnp.ones((tc, GROUP_W), F32), cos_f], 0)
    sin_f = jnp.concatenate([jnp.zeros((tc, GROUP_W), F32), sin_f], 0)
    return jnp.concatenate([cos_f, sin_f], -1)


def _route(idx8, n_tok):
    tk = n_tok * TOP_K
    flat_e = idx8.reshape(tk)
    order = jnp.argsort(flat_e).astype(I32)
    experts = jnp.arange(N_EXPERTS, dtype=I32)
    counts = jnp.sum((flat_e[:, None] == experts[None, :]).astype(I32), axis=0)
    nblk = (counts + MOE_BLOCK - 1) // MOE_BLOCK
    blk_end = jnp.cumsum(nblk)
    blk_start = blk_end - nblk
    grp_start = jnp.cumsum(counts) - counts
    nb = tk // MOE_BLOCK + N_EXPERTS
    bi = jnp.arange(nb, dtype=I32)
    blk_e = jnp.minimum(jnp.sum((blk_end[None, :] <= bi[:, None]).astype(I32), axis=1), N_EXPERTS - 1)
    j = bi - blk_start[blk_e]
    n_valid = jnp.clip(counts[blk_e] - j * MOE_BLOCK, 0, MOE_BLOCK)
    r = jnp.arange(MOE_BLOCK, dtype=I32)
    sorted_pos = (grp_start[blk_e] + j * MOE_BLOCK)[:, None] + r[None, :]
    valid = r[None, :] < n_valid[:, None]
    flat = order[jnp.clip(sorted_pos, 0, tk - 1)]
    tok = flat // TOP_K
    choice = flat - tok * TOP_K
    dst = (tok // COMB_T) * (COMB_T * TOP_K) + choice * COMB_T + tok % COMB_T
    spare = tk + (bi % 2)[:, None] * MOE_BLOCK + r[None, :]
    slot_dst = jnp.where(valid, dst, spare).astype(I32)
    slot_tok = jnp.where(valid, tok, 0).astype(I32)
    first = (tk + MOE_BLOCK + r)[None, :].astype(I32)
    dst_prev = jnp.concatenate([first, slot_dst[:-1]], axis=0)
    return (slot_tok.reshape(nb, 1, MOE_BLOCK), slot_dst.reshape(nb, 1, MOE_BLOCK),
            dst_prev.reshape(nb, 1, MOE_BLOCK), blk_e.astype(I32), tk + 2 * MOE_BLOCK)


def kernel(x, c, ctx, c_ctx, ada_w, ada_b, norm_mix, norm_ffn, norm_final, w_in, w_vres, w_out, rw_mu, rw_w0, rw_w2, rw_a0, rw_a2, rw_g2, rw_kk, rw_ka, rw_rk, rw_ln_w, rw_ln_b, rw_v0, rw_v2, ret_lambda, gdn_conv_w, gdn_a_log, gdn_dt_bias, gdn_norm_w, lru_conv_w, lru_conv_b, lru_w_a, lru_b_a, lru_w_x, lru_b_x, lru_lambda, router_w, router_bias, exp_w_gate, exp_w_up, exp_w_down, sh_w_gate, sh_w_up, sh_w_down):
    B, tx, _ = x.shape
    tc = ctx.shape[1]
    depth = w_in.shape[0]
    assert tc % TM == 0 and tx % TM == 0 and x.shape[2] == D_MODEL and B % SCAN_B == 0
    nc = tc // TM
    ncc = tc // CH
    S = tc + tx
    xs = jnp.concatenate([ctx, x], axis=1)
    cvec = jnp.concatenate([c, c_ctx[None, :], jnp.zeros((8 - B - 1, D_MODEL), F32)], 0)
    rope = _rope_tables(tc, tx)
    expand = _head_expand()
    vfirst = None
    for l in range(depth):
        ada = _ada_call(cvec, ada_w[l], ada_b[l][None, :]).reshape(8, 6, D_MODEL)
        mod = jnp.stack([jnp.broadcast_to(ada[B][None], (B, 6, D_MODEL)), ada[:B]], axis=1)

        w_l = _pack_w_in(w_in[l], None if l == 0 else w_vres[l - 1])
        p_rw, p_ret, p_gdn, p_lru, p_ab = _inproj_call(xs, norm_mix[l][None, :], mod, w_l, nc)

        rprm = _rwkv_params(l, rw_mu, rw_w0, rw_w2, rw_a0, rw_a2, rw_g2, rw_kk, rw_ka, rw_rk, rw_v0, rw_v2)
        sh_rw, dp_rw, bonus, g_rw = _rwkv_prep_call(p_rw, vfirst, rprm, nc)
        if l == 0:
            vfirst = sh_rw
        ys = _rwkv_scan_call(sh_rw, dp_rw, ncc)

        ros = _ret_call(p_ret, rope, _lanes_per_head(-ret_lambda[l]), ncc)

        gprm = {"conv_w": gdn_conv_w[l],
                "alog": jnp.pad(gdn_a_log[l].reshape(1, 2 * N_HEADS), ((0, 0), (0, 128 - 2 * N_HEADS))),
                "dtb": jnp.pad(gdn_dt_bias[l].reshape(1, 2 * N_HEADS), ((0, 0), (0, 128 - 2 * N_HEADS))),
                "expand": expand}
        sh_g, dp_g = _gdn_prep_call(p_gdn, p_ab, gprm, nc)
        gos = _gdn_scan_call(sh_g, dp_g, ncc)

        lprm = {"conv_w": lru_conv_w[l], "conv_b": lru_conv_b[l][None, :],
                "wa": jnp.stack([_block_diag_heads(lru_w_a[l, d]) for d in range(2)]).astype(BF16),
                "ba": lru_b_a[l],
                "wx": jnp.stack([_block_diag_heads(lru_w_x[l, d]) for d in range(2)]).astype(BF16),
                "bx": lru_b_x[l], "lam": lru_lambda[l]}
        lhs = _lru_call(p_lru, lprm, nc)

        xs = _mix_out_call(ys, bonus, g_rw, ros, p_ret, gos, p_gdn, lhs, p_lru, xs, mod, w_out[l].astype(BF16),
                           rw_ln_w[l][None, :], rw_ln_b[l][None, :], jnp.tile(gdn_norm_w[l], N_HEADS)[None, :], nc)

        rw_pad = jnp.pad(router_w[l], ((0, 0), (0, 128 - N_EXPERTS)))
        rb_pad = jnp.concatenate([router_bias[l], jnp.full((128 - N_EXPERTS,), -jnp.inf, F32)])[None, :]
        h2, idx, gate, shared = _moe_pre_call(xs, norm_ffn[l][None, :], mod, rw_pad, rb_pad,
                                              sh_w_gate[l].astype(BF16), sh_w_up[l].astype(BF16),
                                              sh_w_down[l].astype(BF16), nc)
        n_tok = B * S
        slot_tok, slot_dst, dst_prev, blk_e, n_rows = _route(idx[:, :, :TOP_K], n_tok)
        y = _expert_call(blk_e, slot_tok, slot_dst, dst_prev, h2.reshape(n_tok, D_MODEL // 2), exp_w_gate[l].astype(BF16),
                         exp_w_up[l].astype(BF16), exp_w_down[l].astype(BF16), n_rows)
        xs = _combine_call(y, gate, shared, xs, mod, nc, norm_final[None, :] if l == depth - 1 else None)
    return xs
```
